```python
import math, functools
import jax, jax.numpy as jnp
from jax import lax
import numpy as np

D_MODEL = 1024
BATCH = 2
SEQ = 8192
DEPTH = 1
DEC_BATCH = 128
DEC_SEQ = 8
PAST_LEN = 8192
PAGE_SIZE = 128

D_PLE = 256
H_A = 4
DH_A = 64
H_B = 8
DH_B = 64
W_A = H_A * 2 * DH_A
W_B = H_B * DH_B
D_MIX = W_A + W_B
MOBA_BLOCK = 256
MOBA_TOPK = 3
MOBA_Q_BLOCK = 64
Q_BLOCK = 128
N_BUCKETS = 32
MAX_DIST = 128
EPS = 1e-6
SPLITS = (W_A, 2 * W_A, 3 * W_A, 3 * W_A + W_B, 3 * W_A + 2 * W_B, 3 * W_A + 3 * W_B)
D_IN = 3 * W_A + 3 * W_B + D_MIX

kernel_name = "hymba_diff_moba_decoder_step"


def rmsnorm(x, g):
    xf = x.astype(jnp.float32)
    xf = xf * lax.rsqrt(jnp.mean(xf * xf, axis=-1, keepdims=True) + EPS)
    return (xf * g.astype(jnp.float32)).astype(x.dtype)


def rel_bucket(dist):
    n = jnp.maximum(dist, 0)
    max_exact = N_BUCKETS // 2
    nf = jnp.maximum(n, 1).astype(jnp.float32)
    large = max_exact + (jnp.log(nf / max_exact) / math.log(MAX_DIST / max_exact)
                         * (N_BUCKETS - max_exact)).astype(jnp.int32)
    large = jnp.minimum(large, N_BUCKETS - 1)
    return jnp.where(n < max_exact, n, large)


def project(xn, w):
    N, S, _ = xn.shape
    proj = jnp.einsum('nsd,de->nse', xn, w)
    qa, ka, va, qb, kb, vb, z = jnp.split(proj, SPLITS, axis=-1)
    return (qa.reshape(N, S, H_A, 2, DH_A), ka.reshape(N, S, H_A, 2 * DH_A),
            va.reshape(N, S, H_A, 2 * DH_A), qb.reshape(N, S, H_B, DH_B),
            kb.reshape(N, S, H_B, DH_B), vb.reshape(N, S, H_B, DH_B), z)


def diff_attn_core(q, k, v, bias, mask, lam):
    s = jnp.einsum('nqhmd,nkhmd->nhmqk', q, k).astype(jnp.float32) * (DH_A ** -0.5) + bias[:, :, None]
    s = jnp.where(mask, s, -jnp.inf)
    p = jax.nn.softmax(s, axis=-1)
    a = p[:, :, 0] - lam * p[:, :, 1]
    return jnp.einsum('nhqk,nkhe->nqhe', a.astype(v.dtype), v)


def diff_attn_prompt(qa, ka, va, tab_a, lam):
    N, S, H = qa.shape[:3]
    k = ka.reshape(N, S, H, 2, DH_A)
    kpos = jnp.arange(S)
    nqb = S // Q_BLOCK
    qblocks = qa.reshape(N, nqb, Q_BLOCK, H, 2, DH_A).swapaxes(0, 1)

    def step(args):
        i, qi = args
        dist = (i * Q_BLOCK + jnp.arange(Q_BLOCK))[:, None] - kpos[None, :]
        bias = jnp.moveaxis(tab_a[rel_bucket(dist)], -1, 0)[None]
        return diff_attn_core(qi, k, va, bias, dist >= 0, lam)

    out = lax.map(step, (jnp.arange(nqb, dtype=jnp.int32), qblocks))
    return out.swapaxes(0, 1).reshape(N, S, H, 2 * DH_A)


def diff_attn_sample(qa, ka, va, cache_k, cache_v, layer, page_table, tab_a, lam):
    N, T, H = qa.shape[:3]
    past = page_table.shape[1] * cache_k.shape[2]
    kpos = jnp.arange(past + T)
    qpos = past + jnp.arange(T)
    dist = qpos[:, None] - kpos[None, :]
    bias = jnp.moveaxis(tab_a[rel_bucket(dist)], -1, 0)[None]
    mask = dist >= 0

    def step(args):
        qi, kn, vn, pt = args
        kk = jnp.concatenate([cache_k[layer, pt].reshape(past, H, 2 * DH_A), kn], 0)
        kk = kk.reshape(past + T, H, 2, DH_A)
        vv = jnp.concatenate([cache_v[layer, pt].reshape(past, H, 2 * DH_A), vn], 0)
        return diff_attn_core(qi[None], kk[None], vv[None], bias, mask, lam)[0]

    return lax.map(step, (qa, ka, va, page_table))


def moba_blocks(kp, vp):
    N, Lp, H, d = kp.shape
    nb = Lp // MOBA_BLOCK
    kblk = kp.reshape(N, nb, MOBA_BLOCK, H, d)
    kmean = jnp.mean(kblk.astype(jnp.float32), axis=2)
    kblk_t = kblk.transpose(0, 3, 1, 2, 4)
    vblk_t = vp.reshape(N, nb, MOBA_BLOCK, H, d).transpose(0, 3, 1, 2, 4)
    return kblk_t, vblk_t, kmean


def moba_attend(qi, q0, kblk_t, vblk_t, kmean, tab_bh):
    N, Q, H, d = qi.shape
    nb, bs = kblk_t.shape[2], kblk_t.shape[3]
    scale = d ** -0.5
    own = q0 // bs
    qpos = q0 + jnp.arange(Q)
    qh = qi.transpose(0, 2, 1, 3)
    hidx = jnp.arange(H)
    k_own = lax.dynamic_index_in_dim(kblk_t, own, axis=2, keepdims=False)
    v_own = lax.dynamic_index_in_dim(vblk_t, own, axis=2, keepdims=False)
    d_own = qpos[:, None] - (own * bs + jnp.arange(bs))[None, :]
    s_own = (jnp.einsum('nhqd,nhkd->nhqk', qh, k_own).astype(jnp.float32) * scale
             + tab_bh[:, rel_bucket(d_own)])
    s_own = jnp.where(d_own >= 0, s_own, -jnp.inf)
    k_sel = min(MOBA_TOPK, nb - 1)
    if k_sel == 0:
        p_own = jax.nn.softmax(s_own, axis=-1)
        out = jnp.einsum('nhqk,nhkd->nhqd', p_own.astype(v_own.dtype), v_own)
        return out.transpose(0, 2, 1, 3)
    gate = jnp.einsum('nhqd,njhd->nhqj', qh.astype(jnp.float32), kmean)
    gate = jnp.where(jnp.arange(nb) < own, gate, -jnp.inf)
    _, idx = lax.top_k(gate, k_sel)
    nidx = jnp.arange(N)[:, None, None, None]
    ksel = kblk_t[nidx, hidx[None, :, None, None], idx]
    vsel = vblk_t[nidx, hidx[None, :, None, None], idx]
    d_sel = qpos[:, None, None] - (idx[..., None] * bs + jnp.arange(bs))
    s_sel = (jnp.einsum('nhqd,nhqkjd->nhqkj', qh, ksel).astype(jnp.float32) * scale
             + tab_bh[hidx[None, :, None, None, None], rel_bucket(d_sel)])
    s_sel = jnp.where((idx < own)[..., None], s_sel, -jnp.inf)
    s = jnp.concatenate([s_sel.reshape(N, H, Q, k_sel * bs), s_own], axis=-1)
    p = jax.nn.softmax(s, axis=-1)
    p_sel = p[..., :k_sel * bs].reshape(N, H, Q, k_sel, bs).astype(vsel.dtype)
    p_own = p[..., k_sel * bs:].astype(v_own.dtype)
    out = (jnp.einsum('nhqkj,nhqkjd->nhqd', p_sel, vsel)
           + jnp.einsum('nhqk,nhkd->nhqd', p_own, v_own))
    return out.transpose(0, 2, 1, 3)


def moba_prompt(qb, kb, vb, tab_bh):
    N, S, H, d = qb.shape
    nb = -(-S // MOBA_BLOCK)
    pad = nb * MOBA_BLOCK - S
    kp = jnp.pad(kb, ((0, 0), (0, pad), (0, 0), (0, 0)))
    vp = jnp.pad(vb, ((0, 0), (0, pad), (0, 0), (0, 0)))
    kblk_t, vblk_t, kmean = moba_blocks(kp, vp)
    nqb = S // MOBA_Q_BLOCK
    qblocks = qb.reshape(N, nqb, MOBA_Q_BLOCK, H, d).swapaxes(0, 1)

    def step(args):
        i, qi = args
        return moba_attend(qi, i * MOBA_Q_BLOCK, kblk_t, vblk_t, kmean, tab_bh)

    out = lax.map(step, (jnp.arange(nqb, dtype=jnp.int32), qblocks))
    return out.swapaxes(0, 1).reshape(N, S, H, d)


def moba_sample(qb, kb, vb, cache_k, cache_v, layer, page_table, tab_bh):
    N, T, H, d = qb.shape
    past = page_table.shape[1] * cache_k.shape[2]
    L = past + T
    nb = -(-L // MOBA_BLOCK)
    zpad = jnp.zeros((nb * MOBA_BLOCK - L, H, d), kb.dtype)

    def step(args):
        qi, kn, vn, pt = args
        kk = jnp.concatenate([cache_k[layer, pt].reshape(past, H, d), kn, zpad], 0)
        vv = jnp.concatenate([cache_v[layer, pt].reshape(past, H, d), vn, zpad], 0)
        kblk_t, vblk_t, kmean = moba_blocks(kk[None], vv[None])
        return moba_attend(qi[None], past, kblk_t, vblk_t, kmean, tab_bh)[0]

    return lax.map(step, (qb, kb, vb, page_table))


def layer_forward(h, p_l, attend_a, attend_b, lam_init, w_in_l, w_out_l, g_mix_l,
                  subln_g_l, w_ple_l, w_pg_l, g_ple_l):
    N, S = h.shape[:2]
    qa, ka, va, qb, kb, vb, z = project(rmsnorm(h, g_mix_l), w_in_l)
    ya = rmsnorm(attend_a(qa, ka, va), subln_g_l) * (1.0 - lam_init)
    yb = attend_b(qb, kb, vb)
    y = jnp.concatenate([ya.reshape(N, S, W_A), yb.reshape(N, S, W_B)], axis=-1) * jax.nn.silu(z)
    h = h + jnp.einsum('nse,ed->nsd', y, w_out_l)
    gate = jax.nn.sigmoid(jnp.einsum('nsd,de->nse', rmsnorm(h, g_ple_l), w_pg_l)
                          .astype(jnp.float32)).astype(h.dtype)
    h = h + gate * jnp.einsum('nsp,pd->nsd', p_l, w_ple_l)
    return h, ka, va, kb, vb


def setup_inputs(seed: int = 0) -> dict:
    key = jax.random.key(seed)
    ks = jax.random.split(key, 20)
    f32 = jnp.float32
    n_pages = PAST_LEN // PAGE_SIZE
    n_used = DEC_BATCH * n_pages
    n_pool = n_used + (n_used + 3) // 4
    nrm = lambda k, s: jax.random.normal(k, s, f32)
    page_table = jax.random.permutation(ks[6], n_pool)[:n_used].reshape(DEC_BATCH, n_pages).astype(jnp.int32)
    return {
        "x_prompt": nrm(ks[0], (BATCH, SEQ, D_MODEL)),
        "x_sample": nrm(ks[1], (DEC_BATCH, DEC_SEQ, D_MODEL)),
        "cache_diff_k": nrm(ks[2], (DEPTH, n_pool, PAGE_SIZE, H_A, 2 * DH_A)),
        "cache_diff_v": nrm(ks[3], (DEPTH, n_pool, PAGE_SIZE, H_A, 2 * DH_A)),
        "cache_moba_k": nrm(ks[4], (DEPTH, n_pool, PAGE_SIZE, H_B, DH_B)),
        "cache_moba_v": nrm(ks[5], (DEPTH, n_pool, PAGE_SIZE, H_B, DH_B)),
        "page_table": page_table,
        "p_prompt": nrm(ks[7], (DEPTH, BATCH, SEQ, D_PLE)),
        "p_sample": nrm(ks[8], (DEPTH, DEC_BATCH, DEC_SEQ, D_PLE)),
        "w_in": nrm(ks[9], (DEPTH, D_MODEL, D_IN)) * D_MODEL ** -0.5,
        "w_out": nrm(ks[10], (DEPTH, D_MIX, D_MODEL)) * D_MIX ** -0.5,
        "g_mix": 1.0 + 0.02 * nrm(ks[11], (DEPTH, D_MODEL)),
        "lam": 0.1 * nrm(ks[12], (DEPTH, 4, DH_A)),
        "subln_g": 1.0 + 0.02 * nrm(ks[13], (DEPTH, 2 * DH_A)),
        "rel_bias": 0.5 * nrm(ks[14], (N_BUCKETS, H_A + H_B)),
        "w_ple": nrm(ks[15], (DEPTH, D_PLE, D_MODEL)) * D_PLE ** -0.5,
        "w_pg": nrm(ks[16], (DEPTH, D_MODEL, D_MODEL)) * D_MODEL ** -0.5,
        "g_ple": 1.0 + 0.02 * nrm(ks[17], (DEPTH, D_MODEL)),
        "g_final": 1.0 + 0.02 * nrm(ks[18], (D_MODEL,)),
    }


def reference(x_prompt, x_sample, cache_diff_k, cache_diff_v, cache_moba_k, cache_moba_v,
              page_table, p_prompt, p_sample, w_in, w_out, g_mix, lam, subln_g, rel_bias,
              w_ple, w_pg, g_ple, g_final):
    tab_a = rel_bias[:, :H_A]
    tab_bh = rel_bias[:, H_A:].T
    hp, hs = x_prompt, x_sample
    nkap, nvap, nkbp, nvbp, nkas, nvas, nkbs, nvbs = [], [], [], [], [], [], [], []
    for layer in range(DEPTH):
        lam_init = 0.8 - 0.6 * math.exp(-0.3 * layer)
        lf = lam[layer].astype(jnp.float32)
        lam_l = jnp.exp(jnp.sum(lf[0] * lf[1])) - jnp.exp(jnp.sum(lf[2] * lf[3])) + lam_init
        wts = (lam_init, w_in[layer], w_out[layer], g_mix[layer], subln_g[layer],
               w_ple[layer], w_pg[layer], g_ple[layer])
        hp, ka, va, kb, vb = layer_forward(
            hp, p_prompt[layer],
            functools.partial(diff_attn_prompt, tab_a=tab_a, lam=lam_l),
            functools.partial(moba_prompt, tab_bh=tab_bh), *wts)
        nkap.append(ka); nvap.append(va); nkbp.append(kb); nvbp.append(vb)
        hs, ka, va, kb, vb = layer_forward(
            hs, p_sample[layer],
            functools.partial(diff_attn_sample, cache_k=cache_diff_k, cache_v=cache_diff_v,
                              layer=layer, page_table=page_table, tab_a=tab_a, lam=lam_l),
            functools.partial(moba_sample, cache_k=cache_moba_k, cache_v=cache_moba_v,
                              layer=layer, page_table=page_table, tab_bh=tab_bh), *wts)
        nkas.append(ka); nvas.append(va); nkbs.append(kb); nvbs.append(vb)
    y_prompt = rmsnorm(hp, g_final)
    y_sample = rmsnorm(hs, g_final)
    return (y_prompt, y_sample, jnp.stack(nkap), jnp.stack(nvap), jnp.stack(nkbp), jnp.stack(nvbp),
            jnp.stack(nkas), jnp.stack(nvas), jnp.stack(nkbs), jnp.stack(nvbs))
```

```python
import functools
import math

import jax
import jax.numpy as jnp
import numpy as np
from jax import lax
from jax.experimental import pallas as pl
from jax.experimental.pallas import tpu as pltpu

F32 = jnp.float32
BF16 = jnp.bfloat16
NEG_INF = float("-inf")
POS_INF = float("inf")

H_A = 4
DH_A = 64
H_B = 8
DH_B = 64
W_A = H_A * 2 * DH_A
W_B = H_B * DH_B
D_MIX = W_A + W_B
MOBA_BLOCK = 256
MOBA_TOPK = 3
N_BUCKETS = 32
MAX_DIST = 128
EPS = 1e-6
QK_SCALE = DH_A ** -0.5

LANES = 128
V7X_VMEM_BYTES = 64 * 1024 * 1024

ROW_TILE = 256
ATT_TILE = 256
PAGES_PER_STEP = 8
NEW_PAD = 16


COMPILER_SCRATCH_BYTES = 16 * 1024 * 1024


def _vmem_limit(nbytes):
    return int(min(nbytes + COMPILER_SCRATCH_BYTES, V7X_VMEM_BYTES - 8 * 1024 * 1024))


def _bucket_of_dist(n):
    n = np.maximum(n, 0)
    max_exact = N_BUCKETS // 2
    nf = np.maximum(n, 1).astype(np.float32)
    large = max_exact + (np.log(nf / np.float32(max_exact)) / np.float32(math.log(MAX_DIST / max_exact))
                         * np.float32(N_BUCKETS - max_exact)).astype(np.int32)
    large = np.minimum(large, N_BUCKETS - 1)
    return np.where(n < max_exact, n, large).astype(np.int32)


def _inproj_kernel(x_ref, g_ref, w_ref, ka_ref, va_ref, kb_ref, vb_ref, z_ref,
                   qa_s_ref, ka_b_ref, va_b_ref, qb_s_ref, kb_b_ref, vb_b_ref, qb_f_ref, km_ref):
    x = x_ref[...]
    xn = x * lax.rsqrt(jnp.mean(x * x, axis=-1, keepdims=True) + EPS)
    xn = (xn * g_ref[...]).astype(BF16)

    def proj(c0, c1):
        return jnp.dot(xn, w_ref[:, c0:c1], preferred_element_type=F32)

    qa = proj(0, W_A) * QK_SCALE
    qa_s_ref[...] = qa.astype(BF16)
    ka = proj(W_A, 2 * W_A)
    ka_ref[...] = ka
    ka_b_ref[...] = ka.astype(BF16)
    va = proj(2 * W_A, 3 * W_A)
    va_ref[...] = va
    va_b_ref[...] = va.astype(BF16)
    o = 3 * W_A
    qb = proj(o, o + W_B) * QK_SCALE
    qb_s_ref[...] = qb.astype(BF16)
    qb_f_ref[...] = qb
    kb = proj(o + W_B, o + 2 * W_B)
    kb_ref[...] = kb
    kb_b_ref[...] = kb.astype(BF16)
    km_ref[0] = jnp.mean(kb, axis=0, keepdims=True)
    vb = proj(o + 2 * W_B, o + 3 * W_B)
    vb_ref[...] = vb
    vb_b_ref[...] = vb.astype(BF16)
    z_ref[...] = proj(o + 3 * W_B, o + 3 * W_B + D_MIX)


def _inproj(x2, g_mix, w_in_b):
    R, D = x2.shape
    d_in = w_in_b.shape[1]
    ts = ROW_TILE
    nt = R // ts
    row = lambda w: pl.BlockSpec((ts, w), lambda i: (i, 0))
    f32o = lambda w: jax.ShapeDtypeStruct((R, w), F32)
    b16o = lambda w: jax.ShapeDtypeStruct((R, w), BF16)
    out_shape = (f32o(W_A), f32o(W_A), f32o(W_B), f32o(W_B), f32o(D_MIX),
                 b16o(W_A), b16o(W_A), b16o(W_A), b16o(W_B), b16o(W_B), b16o(W_B), f32o(W_B),
                 jax.ShapeDtypeStruct((nt, 1, W_B), F32))
    out_specs = (row(W_A), row(W_A), row(W_B), row(W_B), row(D_MIX),
                 row(W_A), row(W_A), row(W_A), row(W_B), row(W_B), row(W_B), row(W_B),
                 pl.BlockSpec((1, 1, W_B), lambda i: (i, 0, 0)))
    work = 2 * (ts * D * 4 + D * d_in * 2 + ts * (4 * 512 + D_MIX + 512) * 4 + ts * 6 * 512 * 2)
    return pl.pallas_call(
        _inproj_kernel,
        grid=(nt,),
        in_specs=[row(D), pl.BlockSpec((1, D), lambda i: (0, 0)), pl.BlockSpec((D, d_in), lambda i: (0, 0))],
        out_specs=out_specs,
        out_shape=out_shape,
        compiler_params=pltpu.CompilerParams(dimension_semantics=("arbitrary",), vmem_limit_bytes=_vmem_limit(work)),
        name="inproj",
    )(x2, g_mix.reshape(1, D), w_in_b)


def _softmax_tile(s, vt, m_ref, l_ref, acc_ref, colsel):
    m_prev = m_ref[...]
    m_tile = jnp.max(s, axis=0, keepdims=True)
    if colsel is not None:
        m_tile = jnp.where(colsel, m_tile, NEG_INF)
    m_new = jnp.maximum(m_prev, m_tile)
    m_safe = jnp.where(m_new == NEG_INF, 0.0, m_new)
    alpha = jnp.exp(m_prev - m_safe)
    sub = m_safe if colsel is None else jnp.where(colsel, m_safe, POS_INF)
    p = jnp.exp(s - sub)
    l_ref[...] = alpha * l_ref[...] + jnp.sum(p, axis=0, keepdims=True)
    pv = jnp.dot(vt, p.astype(BF16), preferred_element_type=F32)
    acc_ref[...] = acc_ref[...] * alpha + pv
    m_ref[...] = m_new


def _prompt_attn_kernel(*refs, moba, lam_init):
    if moba:
        (qt_ref, k_ref, vt_ref, bias_ref, qf_ref, km_ref, out_ref, m_ref, l_ref, acc_ref, sel_ref) = refs
    else:
        (qt_ref, k_ref, vt_ref, bias_ref, lam_ref, g_ref, out_ref, m_ref, l_ref, acc_ref) = refs
    i = pl.program_id(2)
    tq = qt_ref.shape[-1]
    half = qt_ref.shape[0] // 2

    qt = qt_ref[...]
    feat = lax.broadcasted_iota(jnp.int32, qt.shape, 0)
    zero = jnp.zeros_like(qt)
    q2 = jnp.concatenate([jnp.where(feat < half, qt, zero), jnp.where(feat >= half, qt, zero)], axis=1)

    m_ref[...] = jnp.full(m_ref.shape, NEG_INF, F32)
    l_ref[...] = jnp.zeros(l_ref.shape, F32)
    acc_ref[...] = jnp.zeros(acc_ref.shape, F32)

    if moba:
        nb = km_ref.shape[0]
        km = km_ref[...]
        col = lax.broadcasted_iota(jnp.int32, km.shape, 1)
        qf = qf_ref[...]
        hi = lax.Precision.HIGHEST
        g = jnp.concatenate(
            [jnp.dot(jnp.where(col < half, km, 0.0), qf, precision=hi, preferred_element_type=F32),
             jnp.dot(jnp.where(col >= half, km, 0.0), qf, precision=hi, preferred_element_type=F32)], axis=1)
        blk = lax.broadcasted_iota(jnp.int32, g.shape, 0)
        past = blk < i
        g = jnp.where(past, g, NEG_INF)
        sel = jnp.zeros(g.shape, jnp.bool_)
        for _ in range(min(MOBA_TOPK, nb - 1)):
            mx = jnp.max(g, axis=0, keepdims=True)
            idx = jnp.min(jnp.where(g == mx, blk, nb), axis=0, keepdims=True)
            hit = blk == idx
            sel = jnp.logical_or(sel, jnp.logical_and(hit, past))
            g = jnp.where(hit, NEG_INF, g)
        self32 = sel.astype(F32)
        for r in range(nb):
            sel_ref[r] = self32[r:r + 1, :]

    def far_body(j, carry):
        s = jnp.dot(k_ref[j], q2, preferred_element_type=F32)
        colsel = (sel_ref[j] > 0.5) if moba else None
        _softmax_tile(s, vt_ref[j], m_ref, l_ref, acc_ref, colsel)
        return carry

    lax.fori_loop(0, jnp.maximum(i - 1, 0), far_body, 0)

    @pl.when(i >= 1)
    def _():
        j = i - 1
        s = jnp.dot(k_ref[j], q2, preferred_element_type=F32) + bias_ref[0]
        colsel = (sel_ref[j] > 0.5) if moba else None
        _softmax_tile(s, vt_ref[j], m_ref, l_ref, acc_ref, colsel)

    s = jnp.dot(k_ref[i], q2, preferred_element_type=F32) + bias_ref[1]
    _softmax_tile(s, vt_ref[i], m_ref, l_ref, acc_ref, None)

    acc = acc_ref[...]
    inv = 1.0 / l_ref[...]
    if moba:
        o = jnp.concatenate([acc[:half, :tq] * inv[:, :tq], acc[half:, tq:] * inv[:, tq:]], axis=0)
    else:
        lam = lam_ref[...]
        lam_l = (jnp.exp(jnp.sum(lam[0:1] * lam[1:2], axis=-1, keepdims=True))
                 - jnp.exp(jnp.sum(lam[2:3] * lam[3:4], axis=-1, keepdims=True)) + lam_init)
        o = acc[:, :tq] * inv[:, :tq] - lam_l * (acc[:, tq:] * inv[:, tq:])
        o = o * lax.rsqrt(jnp.mean(o * o, axis=0, keepdims=True) + EPS)
        o = (o * g_ref[...]) * (1.0 - lam_init)
    out_ref[...] = o.T


def _prompt_bias_tiles(tabp_pairs):
    t = ATT_TILE
    kk = np.arange(t)[:, None]
    qq = np.arange(t)[None, :]
    b_prev = _bucket_of_dist(qq + t - kk)
    d = qq - kk
    b_diag = _bucket_of_dist(d)
    prev = tabp_pairs[:, :, b_prev]
    diag = jnp.where(d >= 0, tabp_pairs[:, :, b_diag], NEG_INF)
    both = jnp.stack([prev, diag], axis=1)
    return jnp.concatenate([both[:, :, 0], both[:, :, 1]], axis=-1).astype(F32)


def _tiles_major(a, nt, t, transpose):
    b = a.shape[0]
    a = a.reshape(b, nt, t, 4, LANES)
    return a.transpose(0, 3, 1, 4, 2) if transpose else a.transpose(0, 3, 1, 2, 4)


def _prompt_attn(q_s, k_b, v_b, bias, *, moba, lam_init, lam=None, subln_g=None, q_f=None, kmean=None):
    B, S, _ = q_s.shape
    t = ATT_TILE
    nt = S // t
    qt = _tiles_major(q_s, nt, t, True)
    kt = _tiles_major(k_b, nt, t, False)
    vt = _tiles_major(v_b, nt, t, True)
    sq = pl.Squeezed()
    in_specs = [
        pl.BlockSpec((sq, sq, sq, LANES, t), lambda b, h, i: (b, h, i, 0, 0)),
        pl.BlockSpec((sq, sq, nt, t, LANES), lambda b, h, i: (b, h, 0, 0, 0)),
        pl.BlockSpec((sq, sq, nt, LANES, t), lambda b, h, i: (b, h, 0, 0, 0)),
        pl.BlockSpec((sq, 2, t, 2 * t), lambda b, h, i: (h, 0, 0, 0)),
    ]
    scratch = [pltpu.VMEM((1, 2 * t), F32), pltpu.VMEM((1, 2 * t), F32), pltpu.VMEM((LANES, 2 * t), F32)]
    if moba:
        nb = kmean.shape[1]
        args = (qt, kt, vt, bias, _tiles_major(q_f, nt, t, True), kmean)
        in_specs += [pl.BlockSpec((sq, sq, sq, LANES, t), lambda b, h, i: (b, h, i, 0, 0)),
                     pl.BlockSpec((sq, nb, LANES), lambda b, h, i: (b, 0, h))]
        scratch += [pltpu.VMEM((nb, 1, 2 * t), F32)]
    else:
        args = (qt, kt, vt, bias, lam, subln_g.reshape(LANES, 1))
        in_specs += [pl.BlockSpec(lam.shape, lambda b, h, i: (0, 0)),
                     pl.BlockSpec((LANES, 1), lambda b, h, i: (0, 0))]
    work = 2 * (2 * S * LANES * 2 + 2 * t * 2 * t * 4 + t * LANES * 12) + 8 * t * 2 * t * 4
    return pl.pallas_call(
        functools.partial(_prompt_attn_kernel, moba=moba, lam_init=lam_init),
        grid=(B, 4, nt),
        in_specs=in_specs,
        out_specs=pl.BlockSpec((sq, t, LANES), lambda b, h, i: (b, i, h)),
        out_shape=jax.ShapeDtypeStruct((B, S, 4 * LANES), F32),
        scratch_shapes=scratch,
        compiler_params=pltpu.CompilerParams(dimension_semantics=("arbitrary", "arbitrary", "arbitrary"),
                                             vmem_limit_bytes=_vmem_limit(work)),
        name="prompt_moba" if moba else "prompt_diff",
    )(*args)


def _gather_pages(k_pages, v_pages, kbuf, vbuf):
    page = k_pages[0].shape[0]
    for p, (kp, vp) in enumerate(zip(k_pages, v_pages)):
        kbuf[p * page:(p + 1) * page, :] = kp[...].astype(BF16)
        vbuf[p * page:(p + 1) * page, :] = vp[...].astype(BF16)


def _nt_dot(a, b):
    return lax.dot_general(a, b, (((1,), (1,)), ((), ())), preferred_element_type=F32)


def _row_softmax_tile(s, v, m_ref, l_ref, acc_ref):
    m_prev = m_ref[...]
    m_new = jnp.maximum(m_prev, jnp.max(s, axis=-1, keepdims=True))
    alpha = jnp.exp(m_prev - m_new)
    p = jnp.exp(s - m_new)
    l_ref[...] = alpha * l_ref[...] + jnp.sum(p, axis=-1, keepdims=True)
    acc_ref[...] = acc_ref[...] * alpha + jnp.dot(p.astype(BF16), v, preferred_element_type=F32)
    m_ref[...] = m_new


def _sample_diff_kernel(pt_ref, qbd_ref, *rest, n_pages_step, lam_init):
    g_ = n_pages_step
    k_pages = rest[:g_]
    v_pages = rest[g_:2 * g_]
    (knew_ref, vnew_ref, bnear_ref, bnew_ref, lam_ref, gln_ref, out_ref,
     kbuf, vbuf, m_ref, l_ref, acc_ref) = rest[2 * g_:]
    step = pl.program_id(1)
    last = pl.num_programs(1) - 1

    @pl.when(step == 0)
    def _():
        m_ref[...] = jnp.full(m_ref.shape, NEG_INF, F32)
        l_ref[...] = jnp.zeros(l_ref.shape, F32)
        acc_ref[...] = jnp.zeros(acc_ref.shape, F32)

    _gather_pages(k_pages, v_pages, kbuf, vbuf)
    qbd = qbd_ref[...]
    s = _nt_dot(qbd, kbuf[...])
    s = s + bnear_ref[...] * (step == last).astype(F32)
    _row_softmax_tile(s, vbuf[...], m_ref, l_ref, acc_ref)

    @pl.when(step == last)
    def _():
        s_new = _nt_dot(qbd, knew_ref[...]) + bnew_ref[...]
        _row_softmax_tile(s_new, vnew_ref[...], m_ref, l_ref, acc_ref)
        acc = acc_ref[...]
        inv = 1.0 / l_ref[...]
        lam = lam_ref[...]
        lam_l = (jnp.exp(jnp.sum(lam[0:1] * lam[1:2], axis=-1, keepdims=True))
                 - jnp.exp(jnp.sum(lam[2:3] * lam[3:4], axis=-1, keepdims=True)) + lam_init)
        t = out_ref.shape[0]
        rows = 2 * t
        for h in range(H_A):
            blk = acc[h * rows:(h + 1) * rows, h * LANES:(h + 1) * LANES] * inv[h * rows:(h + 1) * rows]
            o = blk[:t] - lam_l * blk[t:]
            o = o * lax.rsqrt(jnp.mean(o * o, axis=-1, keepdims=True) + EPS)
            out_ref[:, h * LANES:(h + 1) * LANES] = (o * gln_ref[...]) * (1.0 - lam_init)


def _sample_moba_kernel(pt_ref, qbd_ref, qbf_ref, *rest, n_pages_step):
    g_ = n_pages_step
    k_pages = rest[:g_]
    v_pages = rest[g_:2 * g_]
    (knew_ref, vnew_ref, bnear_ref, bnew_ref, out_ref,
     kbuf, vbuf, pm_ref, pl_ref, po_ref, km_ref) = rest[2 * g_:]
    step = pl.program_id(1)
    last = pl.num_programs(1) - 1
    page = k_pages[0].shape[0]
    ppb = MOBA_BLOCK // page
    bps = g_ // ppb
    nb = pm_ref.shape[0]

    _gather_pages(k_pages, v_pages, kbuf, vbuf)
    qbd = qbd_ref[...]
    near = (step == last).astype(F32)
    for jb in range(bps):
        j = step * bps + jb
        lo, hi = jb * MOBA_BLOCK, (jb + 1) * MOBA_BLOCK
        ksum = k_pages[jb * ppb][...]
        for q in range(1, ppb):
            ksum = ksum + k_pages[jb * ppb + q][...]
        km_ref[j] = jnp.sum(ksum, axis=0, keepdims=True) * (1.0 / MOBA_BLOCK)
        s = _nt_dot(qbd, kbuf[lo:hi, :])
        if jb == bps - 1:
            s = s + bnear_ref[...] * near
        m = jnp.max(s, axis=-1, keepdims=True)
        p = jnp.exp(s - m)
        pm_ref[j] = jnp.broadcast_to(m, pm_ref.shape[1:])
        pl_ref[j] = jnp.broadcast_to(jnp.sum(p, axis=-1, keepdims=True), pl_ref.shape[1:])
        po_ref[j] = jnp.dot(p.astype(BF16), vbuf[lo:hi, :], preferred_element_type=F32)

    @pl.when(step == last)
    def _():
        kmv = jnp.concatenate([km_ref[j] for j in range(nb)], axis=0)
        gate = lax.dot_general(qbf_ref[...], kmv, (((1,), (1,)), ((), ())),
                               precision=lax.Precision.HIGHEST, preferred_element_type=F32)
        blk = lax.broadcasted_iota(jnp.int32, gate.shape, 1)
        sel = jnp.zeros(gate.shape, F32)
        for _ in range(min(MOBA_TOPK, nb)):
            mx = jnp.max(gate, axis=-1, keepdims=True)
            idx = jnp.min(jnp.where(gate == mx, blk, nb), axis=-1, keepdims=True)
            hit = blk == idx
            sel = jnp.where(hit, 1.0, sel)
            gate = jnp.where(hit, NEG_INF, gate)
        s_own = _nt_dot(qbd, knew_ref[...]) + bnew_ref[...]
        m = jnp.max(s_own, axis=-1, keepdims=True)
        p_own = jnp.exp(s_own - m)
        l = jnp.sum(p_own, axis=-1, keepdims=True)
        o = jnp.dot(p_own.astype(BF16), vnew_ref[...], preferred_element_type=F32)
        for j in range(nb):
            sj = sel[:, j:j + 1] > 0.5
            mj = jnp.where(sj, pm_ref[j][:, :1], NEG_INF)
            m_new = jnp.maximum(m, mj)
            a_old = jnp.exp(m - m_new)
            a_j = jnp.where(sj, jnp.exp(mj - m_new), 0.0)
            l = a_old * l + a_j * pl_ref[j][:, :1]
            o = a_old * o + a_j * po_ref[j]
            m = m_new
        o = o * (1.0 / l)
        t = out_ref.shape[0]
        lane_head = lax.broadcasted_iota(jnp.int32, (t, W_B), 1) // DH_B
        y = jnp.zeros((t, W_B), F32)
        for h in range(H_B):
            y = jnp.where(lane_head == h, o[h * t:(h + 1) * t, :], y)
        out_ref[...] = y


def _page_spec(p, g_, n_pages, page, width):
    return pl.BlockSpec((pl.Squeezed(), page, width),
                        lambda n, s, pt: (pt[n * n_pages + s * g_ + p], 0, 0))


def _sample_common(page_table, cache_k, cache_v):
    ns, n_pages = page_table.shape
    npool, page = cache_k.shape[0], cache_k.shape[1]
    width = cache_k.shape[2] * cache_k.shape[3]
    g_ = min(PAGES_PER_STEP, n_pages)
    ck = cache_k.reshape(npool, page, width)
    cv = cache_v.reshape(npool, page, width)
    page_specs = ([_page_spec(p, g_, n_pages, page, width) for p in range(g_)]
                  + [_page_spec(p, g_, n_pages, page, width) for p in range(g_)])
    return ns, n_pages, page, width, g_, ck, cv, page_specs


def _pad_new(a):
    ns, t, w = a.shape
    return jnp.pad(a, ((0, 0), (0, NEW_PAD - t), (0, 0))).astype(BF16)


def _sample_diff(qbd, page_table, cache_k, cache_v, knew, vnew, bnear, bnew, lam, subln_g, lam_init):
    ns, n_pages, page, width, g_, ck, cv, page_specs = _sample_common(page_table, cache_k, cache_v)
    t = knew.shape[1]
    rows = qbd.shape[1]
    c = g_ * page
    sq = pl.Squeezed()
    full = lambda a: pl.BlockSpec(a.shape, lambda n, s, pt: (0,) * a.ndim)
    per_seq = lambda a: pl.BlockSpec((sq,) + a.shape[1:], lambda n, s, pt: (n,) + (0,) * (a.ndim - 1))
    knew_p, vnew_p = _pad_new(knew), _pad_new(vnew)
    gln = subln_g.reshape(1, LANES)
    grid_spec = pltpu.PrefetchScalarGridSpec(
        num_scalar_prefetch=1,
        grid=(ns, n_pages // g_),
        in_specs=[per_seq(qbd)] + page_specs + [per_seq(knew_p), per_seq(vnew_p), full(bnear), full(bnew),
                                                 full(lam), full(gln)],
        out_specs=pl.BlockSpec((sq, t, width), lambda n, s, pt: (n, 0, 0)),
        scratch_shapes=[pltpu.VMEM((c, width), BF16), pltpu.VMEM((c, width), BF16),
                        pltpu.VMEM((rows, 1), F32), pltpu.VMEM((rows, 1), F32), pltpu.VMEM((rows, width), F32)],
    )
    work = 2 * 2 * g_ * page * width * 4 + 2 * c * width * 2 + 4 * rows * c * 4
    return pl.pallas_call(
        functools.partial(_sample_diff_kernel, n_pages_step=g_, lam_init=lam_init),
        grid_spec=grid_spec,
        out_shape=jax.ShapeDtypeStruct((ns, t, width), F32),
        compiler_params=pltpu.CompilerParams(dimension_semantics=("arbitrary", "arbitrary"),
                                             vmem_limit_bytes=_vmem_limit(work)),
        name="sample_diff",
    )(page_table.reshape(-1), qbd, *([ck] * g_), *([cv] * g_), knew_p, vnew_p, bnear, bnew, lam, gln)


def _sample_moba(qbd, qbf, page_table, cache_k, cache_v, knew, vnew, bnear, bnew):
    ns, n_pages, page, width, g_, ck, cv, page_specs = _sample_common(page_table, cache_k, cache_v)
    t = knew.shape[1]
    rows = qbd.shape[1]
    c = g_ * page
    nb = n_pages * page // MOBA_BLOCK
    sq = pl.Squeezed()
    full = lambda a: pl.BlockSpec(a.shape, lambda n, s, pt: (0,) * a.ndim)
    per_seq = lambda a: pl.BlockSpec((sq,) + a.shape[1:], lambda n, s, pt: (n,) + (0,) * (a.ndim - 1))
    knew_p, vnew_p = _pad_new(knew), _pad_new(vnew)
    grid_spec = pltpu.PrefetchScalarGridSpec(
        num_scalar_prefetch=1,
        grid=(ns, n_pages // g_),
        in_specs=[per_seq(qbd), per_seq(qbf)] + page_specs + [per_seq(knew_p), per_seq(vnew_p),
                                                              full(bnear), full(bnew)],
        out_specs=pl.BlockSpec((sq, t, width), lambda n, s, pt: (n, 0, 0)),
        scratch_shapes=[pltpu.VMEM((c, width), BF16), pltpu.VMEM((c, width), BF16),
                        pltpu.VMEM((nb, rows, LANES), F32), pltpu.VMEM((nb, rows, LANES), F32),
                        pltpu.VMEM((nb, rows, width), F32), pltpu.VMEM((nb, 1, width), F32)],
    )
    work = (2 * 2 * g_ * page * width * 4 + 2 * c * width * 2 + nb * rows * (width + 2 * LANES) * 4
            + 8 * rows * width * 4)
    return pl.pallas_call(
        functools.partial(_sample_moba_kernel, n_pages_step=g_),
        grid_spec=grid_spec,
        out_shape=jax.ShapeDtypeStruct((ns, t, width), F32),
        compiler_params=pltpu.CompilerParams(dimension_semantics=("arbitrary", "arbitrary"),
                                             vmem_limit_bytes=_vmem_limit(work)),
        name="sample_moba",
    )(page_table.reshape(-1), qbd, qbf, *([ck] * g_), *([cv] * g_), knew_p, vnew_p, bnear, bnew)


def _block_diag_q(q, n_heads):
    ns, t, w = q.shape
    groups = 2 if n_heads == H_A else 1
    n_sl = n_heads * groups
    sl = w // n_sl
    owner = jnp.arange(w) // sl
    rows = jnp.arange(n_sl)
    mask = (owner[None, :] == rows[:, None]).astype(q.dtype)
    out = q[:, None, :, :] * mask[None, :, None, :]
    return out.reshape(ns, n_sl * t, w)


def _sample_bias_tables(tabp, groups, t, past, near_len):
    n_heads = tabp.shape[0]
    tt = np.arange(t)
    c = np.arange(near_len)
    b_near = _bucket_of_dist(near_len + tt[:, None] - c[None, :])
    tn = np.arange(NEW_PAD)
    d_new = tt[:, None] - tn[None, :]
    b_new = _bucket_of_dist(d_new)
    valid = (d_new >= 0) & (tn[None, :] < t)
    near = tabp[:, b_near]
    new = jnp.where(valid[None], tabp[:, b_new], NEG_INF)
    rep = lambda a: jnp.broadcast_to(a[:, None], (n_heads, groups) + a.shape[1:]).reshape(
        n_heads * groups * t, a.shape[-1])
    return rep(near).astype(F32), rep(new).astype(F32)


def _outproj_kernel(x_ref, y_ref, z_ref, p_ref, wo_ref, wpg_ref, wple_ref, gple_ref, gfin_ref, out_ref, *, final):
    z = z_ref[...]
    yg = (y_ref[...] * (z * jax.nn.sigmoid(z))).astype(BF16)
    h = x_ref[...] + jnp.dot(yg, wo_ref[...], preferred_element_type=F32)
    hn = h * lax.rsqrt(jnp.mean(h * h, axis=-1, keepdims=True) + EPS)
    hn = (hn * gple_ref[...]).astype(BF16)
    gate = jax.nn.sigmoid(jnp.dot(hn, wpg_ref[...], preferred_element_type=F32))
    ple = jnp.dot(p_ref[...].astype(BF16), wple_ref[...], preferred_element_type=F32)
    h = h + gate * ple
    if final:
        h = h * lax.rsqrt(jnp.mean(h * h, axis=-1, keepdims=True) + EPS)
        h = h * gfin_ref[...]
    out_ref[...] = h


def _outproj(x2, y2, z2, p2, wo_b, wpg_b, wple_b, g_ple, g_final, final):
    R, D = x2.shape
    ts = ROW_TILE
    row = lambda w: pl.BlockSpec((ts, w), lambda i: (i, 0))
    full = lambda a: pl.BlockSpec(a.shape, lambda i: (0, 0))
    gp, gf = g_ple.reshape(1, D), g_final.reshape(1, D)
    work = 2 * (ts * (2 * D + 2 * D_MIX + p2.shape[1]) * 4 + (D_MIX * D + D * D + p2.shape[1] * D) * 2)
    return pl.pallas_call(
        functools.partial(_outproj_kernel, final=final),
        grid=(R // ts,),
        in_specs=[row(D), row(D_MIX), row(D_MIX), row(p2.shape[1]), full(wo_b), full(wpg_b), full(wple_b),
                  full(gp), full(gf)],
        out_specs=row(D),
        out_shape=jax.ShapeDtypeStruct((R, D), F32),
        compiler_params=pltpu.CompilerParams(dimension_semantics=("arbitrary",), vmem_limit_bytes=_vmem_limit(work)),
        name="outproj",
    )(x2, y2, z2, p2, wo_b, wpg_b, wple_b, gp, gf)


def kernel(x_prompt, x_sample, cache_diff_k, cache_diff_v, cache_moba_k, cache_moba_v, page_table, p_prompt, p_sample, w_in, w_out, g_mix, lam, subln_g, rel_bias, w_ple, w_pg, g_ple, g_final):
    depth = w_in.shape[0]
    B, S, D = x_prompt.shape
    NS, T, _ = x_sample.shape
    n_pages = page_table.shape[1]
    page = cache_diff_k.shape[2]
    past = n_pages * page
    assert S % ATT_TILE == 0 and ATT_TILE == MOBA_BLOCK == ROW_TILE
    assert (B * S) % ROW_TILE == 0 and (NS * T) % ROW_TILE == 0
    assert past % MOBA_BLOCK == 0 and MOBA_BLOCK % page == 0 and T <= page and T <= NEW_PAD
    assert n_pages % min(PAGES_PER_STEP, n_pages) == 0 and (min(PAGES_PER_STEP, n_pages) * page) % MOBA_BLOCK == 0
    assert past // MOBA_BLOCK >= MOBA_TOPK

    tabp = (rel_bias - rel_bias[N_BUCKETS - 1:N_BUCKETS, :]).T.astype(F32)
    tab_a, tab_b = tabp[:H_A], tabp[H_A:]
    bias_pd = _prompt_bias_tiles(jnp.stack([tab_a, tab_a], axis=1))
    bias_pm = _prompt_bias_tiles(tab_b.reshape(H_B // 2, 2, N_BUCKETS))
    gpc = min(PAGES_PER_STEP, n_pages) * page
    near_d, new_d = _sample_bias_tables(tab_a, 2, T, past, page)
    near_d = jnp.pad(near_d, ((0, 0), (gpc - page, 0)))
    near_m, new_m = _sample_bias_tables(tab_b, 1, T, past, MOBA_BLOCK)

    hp, hs = x_prompt, x_sample
    outs = [[] for _ in range(8)]
    for layer in range(depth):
        lam_init = 0.8 - 0.6 * math.exp(-0.3 * layer)
        final = layer == depth - 1
        w_in_b = w_in[layer].astype(BF16)
        wo_b, wpg_b, wple_b = w_out[layer].astype(BF16), w_pg[layer].astype(BF16), w_ple[layer].astype(BF16)

        (ka, va, kb, vb, z, qa_s, ka_b, va_b, qb_s, kb_b, vb_b, qb_f, km) = _inproj(
            hp.reshape(B * S, D), g_mix[layer], w_in_b)
        r3 = lambda a: a.reshape(B, S, a.shape[-1])
        ya = _prompt_attn(r3(qa_s), r3(ka_b), r3(va_b), bias_pd, moba=False, lam_init=lam_init,
                          lam=lam[layer], subln_g=subln_g[layer])
        yb = _prompt_attn(r3(qb_s), r3(kb_b), r3(vb_b), bias_pm, moba=True, lam_init=lam_init,
                          q_f=r3(qb_f), kmean=km.reshape(B, S // MOBA_BLOCK, W_B))
        y = jnp.concatenate([ya, yb], axis=-1).reshape(B * S, D_MIX)
        hp = _outproj(hp.reshape(B * S, D), y, z, p_prompt[layer].reshape(B * S, -1), wo_b, wpg_b, wple_b,
                      g_ple[layer], g_final, final).reshape(B, S, D)
        outs[0].append(ka.reshape(B, S, H_A, 2 * DH_A)); outs[1].append(va.reshape(B, S, H_A, 2 * DH_A))
        outs[2].append(kb.reshape(B, S, H_B, DH_B)); outs[3].append(vb.reshape(B, S, H_B, DH_B))

        (ka, va, kb, vb, z, qa_s, _, _, qb_s, _, _, qb_f, _) = _inproj(hs.reshape(NS * T, D), g_mix[layer], w_in_b)
        s3 = lambda a: a.reshape(NS, T, a.shape[-1])
        ya = _sample_diff(_block_diag_q(s3(qa_s), H_A), page_table, cache_diff_k[layer], cache_diff_v[layer],
                          s3(ka), s3(va), near_d, new_d, lam[layer], subln_g[layer], lam_init)
        yb = _sample_moba(_block_diag_q(s3(qb_s), H_B), _block_diag_q(s3(qb_f), H_B), page_table,
                          cache_moba_k[layer], cache_moba_v[layer], s3(kb), s3(vb), near_m, new_m)
        y = jnp.concatenate([ya, yb], axis=-1).reshape(NS * T, D_MIX)
        hs = _outproj(hs.reshape(NS * T, D), y, z, p_sample[layer].reshape(NS * T, -1), wo_b, wpg_b, wple_b,
                      g_ple[layer], g_final, final).reshape(NS, T, D)
        outs[4].append(ka.reshape(NS, T, H_A, 2 * DH_A)); outs[5].append(va.reshape(NS, T, H_A, 2 * DH_A))
        outs[6].append(kb.reshape(NS, T, H_B, DH_B)); outs[7].append(vb.reshape(NS, T, H_B, DH_B))

    return (hp, hs) + tuple(jnp.stack(o) for o in outs)
```

```python
import functools
import math

import jax
import jax.numpy as jnp
import numpy as np
from jax import lax
from jax.experimental import pallas as pl
from jax.experimental.pallas import tpu as pltpu

F32 = jnp.float32
BF16 = jnp.bfloat16
NEG_INF = float("-inf")
POS_INF = float("inf")

H_A = 4
DH_A = 64
H_B = 8
DH_B = 64
W_A = H_A * 2 * DH_A
W_B = H_B * DH_B
D_MIX = W_A + W_B
MOBA_BLOCK = 256
MOBA_TOPK = 3
N_BUCKETS = 32
MAX_DIST = 128
EPS = 1e-6
LOG2E = math.log2(math.e)
Q_SCALE = DH_A ** -0.5 * LOG2E

LANES = 128
BF16_SUBLANES = 16
V7X_VMEM_BYTES = 64 * 1024 * 1024
COMPILER_SCRATCH_BYTES = 16 * 1024 * 1024

ROW_TILE = 256
ATT_TILE = 256
HEAD_STREAMS = 4
PAGES_PER_STEP = 8
NEW_PAD = BF16_SUBLANES


def _vmem_limit(nbytes):
    return int(min(nbytes + COMPILER_SCRATCH_BYTES, V7X_VMEM_BYTES - 8 * 1024 * 1024))


def _bucket_of_dist(n):
    n = np.maximum(n, 0)
    max_exact = N_BUCKETS // 2
    nf = np.maximum(n, 1).astype(np.float32)
    large = max_exact + (np.log(nf / np.float32(max_exact)) / np.float32(math.log(MAX_DIST / max_exact))
                         * np.float32(N_BUCKETS - max_exact)).astype(np.int32)
    large = np.minimum(large, N_BUCKETS - 1)
    return np.where(n < max_exact, n, large).astype(np.int32)


def _bucket_upper_bounds():
    b = _bucket_of_dist(np.arange(4 * MAX_DIST))
    return [(k, int(np.nonzero(b <= k)[0].max())) for k in range(N_BUCKETS - 1)]


BUCKET_HI = _bucket_upper_bounds()


def _inproj_kernel(x_ref, g_ref, w_ref, ka_ref, va_ref, kb_ref, vb_ref, z_ref,
                   qa_s_ref, ka_b_ref, va_b_ref, qb_s_ref, kb_b_ref, vb_b_ref, qb_f_ref, km_ref):
    x = x_ref[...]
    xn = x * lax.rsqrt(jnp.mean(x * x, axis=-1, keepdims=True) + EPS)
    xn = (xn * g_ref[...]).astype(BF16)

    def proj(c0, c1):
        return jnp.dot(xn, w_ref[:, c0:c1], preferred_element_type=F32)

    qa = proj(0, W_A) * Q_SCALE
    qa_s_ref[...] = qa.astype(BF16)
    ka = proj(W_A, 2 * W_A)
    ka_ref[...] = ka
    ka_b_ref[...] = ka.astype(BF16)
    va = proj(2 * W_A, 3 * W_A)
    va_ref[...] = va
    va_b_ref[...] = va.astype(BF16)
    o = 3 * W_A
    qb = proj(o, o + W_B) * Q_SCALE
    qb_s_ref[...] = qb.astype(BF16)
    qb_f_ref[...] = qb
    kb = proj(o + W_B, o + 2 * W_B)
    kb_ref[...] = kb
    kb_b_ref[...] = kb.astype(BF16)
    km_ref[0] = jnp.mean(kb, axis=0, keepdims=True)
    vb = proj(o + 2 * W_B, o + 3 * W_B)
    vb_ref[...] = vb
    vb_b_ref[...] = vb.astype(BF16)
    z_ref[...] = proj(o + 3 * W_B, o + 3 * W_B + D_MIX)


def _inproj(x2, g_mix, w_in_b):
    R, D = x2.shape
    d_in = w_in_b.shape[1]
    ts = ROW_TILE
    nt = R // ts
    row = lambda w: pl.BlockSpec((ts, w), lambda i: (i, 0))
    f32o = lambda w: jax.ShapeDtypeStruct((R, w), F32)
    b16o = lambda w: jax.ShapeDtypeStruct((R, w), BF16)
    out_shape = (f32o(W_A), f32o(W_A), f32o(W_B), f32o(W_B), f32o(D_MIX),
                 b16o(W_A), b16o(W_A), b16o(W_A), b16o(W_B), b16o(W_B), b16o(W_B), f32o(W_B),
                 jax.ShapeDtypeStruct((nt, 1, W_B), F32))
    out_specs = (row(W_A), row(W_A), row(W_B), row(W_B), row(D_MIX),
                 row(W_A), row(W_A), row(W_A), row(W_B), row(W_B), row(W_B), row(W_B),
                 pl.BlockSpec((1, 1, W_B), lambda i: (i, 0, 0)))
    work = 2 * (ts * D * 4 + D * d_in * 2 + ts * (4 * 512 + D_MIX + 512) * 4 + ts * 6 * 512 * 2)
    return pl.pallas_call(
        _inproj_kernel,
        grid=(nt,),
        in_specs=[row(D), pl.BlockSpec((1, D), lambda i: (0, 0)), pl.BlockSpec((D, d_in), lambda i: (0, 0))],
        out_specs=out_specs,
        out_shape=out_shape,
        compiler_params=pltpu.CompilerParams(dimension_semantics=("arbitrary",), vmem_limit_bytes=_vmem_limit(work)),
        name="inproj",
    )(x2, g_mix.reshape(1, D), w_in_b)


def _bias_of_dist(dist, tab_ref, h):
    v = jnp.full(dist.shape, tab_ref[h, N_BUCKETS - 1], F32)
    for b, hi in reversed(BUCKET_HI):
        v = jnp.where(dist <= hi, tab_ref[h, b], v)
    return v


def _softmax_tile(s, vt, m_ref, acc_ref, colsel):
    m_prev = m_ref[...]
    m_tile = jnp.max(s, axis=0, keepdims=True)
    if colsel is not None:
        m_tile = jnp.where(colsel, m_tile, NEG_INF)
    m_new = jnp.maximum(m_prev, m_tile)
    m_safe = jnp.where(m_new == NEG_INF, 0.0, m_new)
    alpha = jnp.exp2(m_prev - m_safe)
    sub = m_safe if colsel is None else jnp.where(colsel, m_safe, POS_INF)
    p = jnp.exp2(s - sub).astype(BF16)
    acc_ref[...] = acc_ref[...] * alpha + jnp.dot(vt, p, preferred_element_type=F32)
    m_ref[...] = m_new


def _prompt_attn_kernel(*refs, moba, lam_init):
    if moba:
        (tab_ref, qt_ref, k_ref, vt_ref, qf_ref, km_ref, out_ref, m_ref, acc_ref, bias_ref, sel_ref) = refs
    else:
        (tab_ref, qt_ref, k_ref, vt_ref, lam_ref, g_ref, out_ref, m_ref, acc_ref, bias_ref) = refs
    hg = pl.program_id(1)
    i = pl.program_id(2)
    ns = qt_ref.shape[0]
    tq = qt_ref.shape[-1]
    tk = k_ref.shape[-2]
    half = qt_ref.shape[1] // 2

    @pl.when(i == 0)
    def _():
        kk = lax.broadcasted_iota(jnp.int32, (tk, tq), 0)
        qq = lax.broadcasted_iota(jnp.int32, (tk, tq), 1)
        for s in range(ns):
            hh = hg * ns + s
            heads = (2 * hh, 2 * hh + 1) if moba else (hh, hh)
            for c, h in enumerate(heads):
                if c == 1 and heads[1] is heads[0]:
                    bias_ref[s, 0, :, tq:] = bias_ref[s, 0, :, :tq]
                    bias_ref[s, 1, :, tq:] = bias_ref[s, 1, :, :tq]
                    continue
                bias_ref[s, 0, :, c * tq:(c + 1) * tq] = _bias_of_dist(qq + tk - kk, tab_ref, h)
                d = qq - kk
                bias_ref[s, 1, :, c * tq:(c + 1) * tq] = jnp.where(d >= 0, _bias_of_dist(d, tab_ref, h), NEG_INF)

    q2 = []
    for s in range(ns):
        qt = qt_ref[s]
        feat = lax.broadcasted_iota(jnp.int32, qt.shape, 0)
        zero = jnp.zeros_like(qt)
        q2.append(jnp.concatenate([jnp.where(feat < half, qt, zero), jnp.where(feat >= half, qt, zero)], axis=1))

    m_ref[...] = jnp.full(m_ref.shape, NEG_INF, F32)
    acc_ref[...] = jnp.zeros(acc_ref.shape, F32)

    if moba:
        nb = km_ref.shape[1]
        hi = lax.Precision.HIGHEST
        for s in range(ns):
            km = km_ref[s]
            col = lax.broadcasted_iota(jnp.int32, km.shape, 1)
            qf = qf_ref[s]
            g = jnp.concatenate(
                [jnp.dot(jnp.where(col < half, km, 0.0), qf, precision=hi, preferred_element_type=F32),
                 jnp.dot(jnp.where(col >= half, km, 0.0), qf, precision=hi, preferred_element_type=F32)], axis=1)
            blk = lax.broadcasted_iota(jnp.int32, g.shape, 0)
            past = blk < i
            g = jnp.where(past, g, NEG_INF)
            sel = jnp.zeros(g.shape, F32)
            for _ in range(min(MOBA_TOPK, nb - 1)):
                mx = jnp.max(g, axis=0, keepdims=True)
                idx = jnp.min(jnp.where(g == mx, blk, nb), axis=0, keepdims=True)
                hit = blk == idx
                sel = jnp.where(jnp.logical_and(hit, past), 1.0, sel)
                g = jnp.where(hit, NEG_INF, g)
            for r in range(nb):
                sel_ref[s, r] = sel[r:r + 1, :]

    def update(j, bias_idx, masked):
        scores = [jnp.dot(k_ref[s, j], q2[s], preferred_element_type=F32) for s in range(ns)]
        for s in range(ns):
            sc = scores[s] if bias_idx is None else scores[s] + bias_ref[s, bias_idx]
            colsel = (sel_ref[s, j] > 0.5) if (moba and masked) else None
            _softmax_tile(sc, vt_ref[s, j], m_ref.at[s], acc_ref.at[s], colsel)

    def far_body(j, carry):
        update(j, None, True)
        return carry

    lax.fori_loop(0, jnp.maximum(i - 1, 0), far_body, 0)

    @pl.when(i >= 1)
    def _():
        update(i - 1, 0, True)

    update(i, 1, False)

    if not moba:
        lam = lam_ref[...]
        lam_l = (jnp.exp(jnp.sum(lam[0:1] * lam[1:2], axis=-1, keepdims=True))
                 - jnp.exp(jnp.sum(lam[2:3] * lam[3:4], axis=-1, keepdims=True)) + lam_init)
    feat_w = 2 * half
    for s in range(ns):
        acc = acc_ref[s]
        inv = 1.0 / acc[feat_w:feat_w + 1, :]
        if moba:
            o = jnp.concatenate([acc[:half, :tq] * inv[:, :tq], acc[half:feat_w, tq:] * inv[:, tq:]], axis=0)
        else:
            o = acc[:feat_w, :tq] * inv[:, :tq] - lam_l * (acc[:feat_w, tq:] * inv[:, tq:])
            o = o * lax.rsqrt(jnp.mean(o * o, axis=0, keepdims=True) + EPS)
            o = (o * g_ref[...]) * (1.0 - lam_init)
        out_ref[:, s * feat_w:(s + 1) * feat_w] = o.T


def _tiles_major(a, nt, t, transpose):
    b = a.shape[0]
    a = a.reshape(b, nt, t, 4, LANES)
    return a.transpose(0, 3, 1, 4, 2) if transpose else a.transpose(0, 3, 1, 2, 4)


def _prompt_attn(q_s, k_b, v_b, tab, *, moba, lam_init, lam=None, subln_g=None, q_f=None, kmean=None):
    B, S, _ = q_s.shape
    t = ATT_TILE
    nt = S // t
    ns = HEAD_STREAMS
    qt = _tiles_major(q_s, nt, t, True)
    kt = _tiles_major(k_b, nt, t, False)
    vt = _tiles_major(v_b, nt, t, True)
    ones = jnp.zeros((B, 4, nt, BF16_SUBLANES, t), BF16).at[:, :, :, 0, :].set(1.0)
    vt = jnp.concatenate([vt, ones], axis=3)
    vrows = LANES + BF16_SUBLANES
    sq = pl.Squeezed()
    in_specs = [
        pl.BlockSpec(memory_space=pltpu.SMEM),
        pl.BlockSpec((sq, ns, sq, LANES, t), lambda b, h, i: (b, h, i, 0, 0)),
        pl.BlockSpec((sq, ns, nt, t, LANES), lambda b, h, i: (b, h, 0, 0, 0), pipeline_mode=pl.Buffered(1)),
        pl.BlockSpec((sq, ns, nt, vrows, t), lambda b, h, i: (b, h, 0, 0, 0), pipeline_mode=pl.Buffered(1)),
    ]
    scratch = [pltpu.VMEM((ns, 1, 2 * t), F32), pltpu.VMEM((ns, vrows, 2 * t), F32),
               pltpu.VMEM((ns, 2, t, 2 * t), F32)]
    if moba:
        nb = kmean.shape[1]
        kmh = kmean.reshape(B, nb, 4, LANES).transpose(0, 2, 1, 3)
        args = (tab, qt, kt, vt, _tiles_major(q_f, nt, t, True), kmh)
        in_specs += [pl.BlockSpec((sq, ns, sq, LANES, t), lambda b, h, i: (b, h, i, 0, 0)),
                     pl.BlockSpec((sq, ns, nb, LANES), lambda b, h, i: (b, h, 0, 0))]
        scratch += [pltpu.VMEM((ns, nb, 1, 2 * t), F32)]
    else:
        args = (tab, qt, kt, vt, lam, subln_g.reshape(LANES, 1))
        in_specs += [pl.BlockSpec(lam.shape, lambda b, h, i: (0, 0)),
                     pl.BlockSpec((LANES, 1), lambda b, h, i: (0, 0))]
    work = (ns * (S * LANES * 2 + S * vrows * 2) + 2 * ns * (2 * t * LANES * 6 + t * LANES * 4)
            + ns * (2 * t * 2 * t * 4 + vrows * 2 * t * 4 + 32 * 8 * 2 * t * 4) + 8 * t * 2 * t * 4)
    return pl.pallas_call(
        functools.partial(_prompt_attn_kernel, moba=moba, lam_init=lam_init),
        grid=(B, 4 // ns, nt),
        in_specs=in_specs,
        out_specs=pl.BlockSpec((sq, t, ns * LANES), lambda b, h, i: (b, i, h)),
        out_shape=jax.ShapeDtypeStruct((B, S, 4 * LANES), F32),
        scratch_shapes=scratch,
        compiler_params=pltpu.CompilerParams(dimension_semantics=("arbitrary", "arbitrary", "arbitrary"),
                                             vmem_limit_bytes=_vmem_limit(work)),
        name="prompt_moba" if moba else "prompt_diff",
    )(*args)


def _nt_dot(a, b):
    return lax.dot_general(a, b, (((1,), (1,)), ((), ())), preferred_element_type=F32)


def _row_softmax_tile(s, v, m_ref, l_ref, acc_ref):
    m_prev = m_ref[...]
    m_new = jnp.maximum(m_prev, jnp.max(s, axis=-1, keepdims=True))
    alpha = jnp.exp2(m_prev - m_new)
    p = jnp.exp2(s - m_new)
    l_ref[...] = alpha * l_ref[...] + jnp.sum(p, axis=-1, keepdims=True)
    acc_ref[...] = acc_ref[...] * alpha + jnp.dot(p.astype(BF16), v, preferred_element_type=F32)
    m_ref[...] = m_new


def _sample_diff_kernel(pt_ref, qbd_ref, *rest, n_pages_step, lam_init):
    g_ = n_pages_step
    k_pages = rest[:g_]
    v_pages = rest[g_:2 * g_]
    (knew_ref, vnew_ref, bnear_ref, bnew_ref, lam_ref, gln_ref, out_ref,
     kbuf, vbuf, m_ref, l_ref, acc_ref) = rest[2 * g_:]
    step = pl.program_id(1)
    last = pl.num_programs(1) - 1
    page = k_pages[0].shape[0] // H_A

    @pl.when(step == 0)
    def _():
        m_ref[...] = jnp.full(m_ref.shape, NEG_INF, F32)
        l_ref[...] = jnp.zeros(l_ref.shape, F32)
        acc_ref[...] = jnp.zeros(acc_ref.shape, F32)

    for p in range(g_):
        for h in range(H_A):
            rows = pl.ds(h, page, stride=H_A)
            kbuf[p * page:(p + 1) * page, h * LANES:(h + 1) * LANES] = k_pages[p][rows, :].astype(BF16)
            vbuf[p * page:(p + 1) * page, h * LANES:(h + 1) * LANES] = v_pages[p][rows, :].astype(BF16)
    qbd = qbd_ref[...]
    s = _nt_dot(qbd, kbuf[...])
    s = s + bnear_ref[...] * (step == last).astype(F32)
    _row_softmax_tile(s, vbuf[...], m_ref, l_ref, acc_ref)

    @pl.when(step == last)
    def _():
        s_new = _nt_dot(qbd, knew_ref[...]) + bnew_ref[...]
        _row_softmax_tile(s_new, vnew_ref[...], m_ref, l_ref, acc_ref)
        acc = acc_ref[...]
        inv = 1.0 / l_ref[...]
        lam = lam_ref[...]
        lam_l = (jnp.exp(jnp.sum(lam[0:1] * lam[1:2], axis=-1, keepdims=True))
                 - jnp.exp(jnp.sum(lam[2:3] * lam[3:4], axis=-1, keepdims=True)) + lam_init)
        t = out_ref.shape[0]
        rows = 2 * t
        for h in range(H_A):
            blk = acc[h * rows:(h + 1) * rows, h * LANES:(h + 1) * LANES] * inv[h * rows:(h + 1) * rows]
            o = blk[:t] - lam_l * blk[t:]
            o = o * lax.rsqrt(jnp.mean(o * o, axis=-1, keepdims=True) + EPS)
            out_ref[:, h * LANES:(h + 1) * LANES] = (o * gln_ref[...]) * (1.0 - lam_init)


def _sample_moba_kernel(pt_ref, qbd_ref, qbf_ref, *rest, n_pages_step):
    g_ = n_pages_step
    k_pages = rest[:g_]
    v_pages = rest[g_:2 * g_]
    (knew_ref, vnew_ref, bnear_ref, bnew_ref, out_ref,
     kbuf, vbuf, pm_ref, pl_ref, po_ref, km_ref) = rest[2 * g_:]
    step = pl.program_id(1)
    last = pl.num_programs(1) - 1
    page = k_pages[0].shape[1]
    ppb = MOBA_BLOCK // page
    bps = g_ // ppb
    nb = po_ref.shape[0]

    @pl.when(step == 0)
    def _():
        km_ref[...] = jnp.zeros(km_ref.shape, F32)
        pm_ref[...] = jnp.full(pm_ref.shape, NEG_INF, F32)
        pl_ref[...] = jnp.zeros(pl_ref.shape, F32)

    for p in range(g_):
        kbuf[:, p * page:(p + 1) * page] = k_pages[p][...].astype(BF16)
        vbuf[:, p * page:(p + 1) * page] = v_pages[p][...].astype(BF16)
    qbd = qbd_ref[...]
    near = (step == last).astype(F32)
    lane = lax.broadcasted_iota(jnp.int32, km_ref.shape, 1)
    for jb in range(bps):
        j = step * bps + jb
        lo, hi = jb * MOBA_BLOCK, (jb + 1) * MOBA_BLOCK
        ksum = k_pages[jb * ppb][...]
        for q in range(1, ppb):
            ksum = ksum + k_pages[jb * ppb + q][...]
        kmean = jnp.sum(ksum, axis=-1, keepdims=True) * (1.0 / MOBA_BLOCK)
        km_ref[...] = jnp.where(lane == j, kmean, km_ref[...])
        s = jnp.dot(qbd, kbuf[:, lo:hi], preferred_element_type=F32)
        if jb == bps - 1:
            s = s + bnear_ref[...] * near
        m = jnp.max(s, axis=-1, keepdims=True)
        p = jnp.exp2(s - m)
        stat_lane = lax.broadcasted_iota(jnp.int32, pm_ref.shape, 1)
        pm_ref[...] = jnp.where(stat_lane == j, m, pm_ref[...])
        pl_ref[...] = jnp.where(stat_lane == j, jnp.sum(p, axis=-1, keepdims=True), pl_ref[...])
        po_ref[j] = _nt_dot(p.astype(BF16), vbuf[:, lo:hi])

    @pl.when(step == last)
    def _():
        gate = jnp.dot(qbf_ref[...], km_ref[...], precision=lax.Precision.HIGHEST, preferred_element_type=F32)
        blk = lax.broadcasted_iota(jnp.int32, gate.shape, 1)
        gate = jnp.where(blk < nb, gate, NEG_INF)
        sel = jnp.zeros(gate.shape, F32)
        for _ in range(min(MOBA_TOPK, nb)):
            mx = jnp.max(gate, axis=-1, keepdims=True)
            idx = jnp.min(jnp.where(gate == mx, blk, LANES), axis=-1, keepdims=True)
            hit = blk == idx
            sel = jnp.where(hit, 1.0, sel)
            gate = jnp.where(hit, NEG_INF, gate)
        s_own = _nt_dot(qbd, knew_ref[...]) + bnew_ref[...]
        m_own = jnp.max(s_own, axis=-1, keepdims=True)
        p_own = jnp.exp2(s_own - m_own)
        chosen = sel > 0.5
        m = jnp.maximum(m_own, jnp.max(jnp.where(chosen, pm_ref[...], NEG_INF), axis=-1, keepdims=True))
        w = jnp.where(chosen, jnp.exp2(pm_ref[...] - m), 0.0)
        a_own = jnp.exp2(m_own - m)
        l = a_own * jnp.sum(p_own, axis=-1, keepdims=True) + jnp.sum(w * pl_ref[...], axis=-1, keepdims=True)
        o = a_own * jnp.dot(p_own.astype(BF16), vnew_ref[...], preferred_element_type=F32)
        for j in range(nb):
            o = o + w[:, j:j + 1] * po_ref[j]
        o = o * (1.0 / l)
        t = out_ref.shape[0]
        lane_head = lax.broadcasted_iota(jnp.int32, (t, W_B), 1) // DH_B
        y = jnp.zeros((t, W_B), F32)
        for h in range(H_B):
            y = jnp.where(lane_head == h, o[h * t:(h + 1) * t, :], y)
        out_ref[...] = y


def _page_specs(g_, n_pages, block):
    def spec(p):
        return pl.BlockSpec((pl.Squeezed(),) + block, lambda n, s, pt: (pt[n * n_pages + s * g_ + p], 0, 0))
    return [spec(p) for p in range(g_)] + [spec(p) for p in range(g_)]


def _pad_new(a):
    ns, t, w = a.shape
    return jnp.pad(a, ((0, 0), (0, NEW_PAD - t), (0, 0))).astype(BF16)


def _sample_diff(qbd, page_table, cache_k, cache_v, knew, vnew, bnear, bnew, lam, subln_g, lam_init):
    ns, n_pages = page_table.shape
    npool, page, nh, dh = cache_k.shape
    width = nh * dh
    g_ = min(PAGES_PER_STEP, n_pages)
    ck = cache_k.reshape(npool, page * nh, dh)
    cv = cache_v.reshape(npool, page * nh, dh)
    t = knew.shape[1]
    rows = qbd.shape[1]
    c = g_ * page
    sq = pl.Squeezed()
    full = lambda a: pl.BlockSpec(a.shape, lambda n, s, pt: (0,) * a.ndim)
    per_seq = lambda a: pl.BlockSpec((sq,) + a.shape[1:], lambda n, s, pt: (n,) + (0,) * (a.ndim - 1))
    knew_p, vnew_p = _pad_new(knew), _pad_new(vnew)
    gln = subln_g.reshape(1, LANES)
    grid_spec = pltpu.PrefetchScalarGridSpec(
        num_scalar_prefetch=1,
        grid=(ns, n_pages // g_),
        in_specs=[per_seq(qbd)] + _page_specs(g_, n_pages, (page * nh, dh))
                 + [per_seq(knew_p), per_seq(vnew_p), full(bnear), full(bnew), full(lam), full(gln)],
        out_specs=pl.BlockSpec((sq, t, width), lambda n, s, pt: (n, 0, 0)),
        scratch_shapes=[pltpu.VMEM((c, width), BF16), pltpu.VMEM((c, width), BF16),
                        pltpu.VMEM((rows, 1), F32), pltpu.VMEM((rows, 1), F32), pltpu.VMEM((rows, width), F32)],
    )
    work = 2 * 2 * g_ * page * width * 4 + 2 * c * width * 2 + 4 * rows * c * 4
    return pl.pallas_call(
        functools.partial(_sample_diff_kernel, n_pages_step=g_, lam_init=lam_init),
        grid_spec=grid_spec,
        out_shape=jax.ShapeDtypeStruct((ns, t, width), F32),
        compiler_params=pltpu.CompilerParams(dimension_semantics=("arbitrary", "arbitrary"),
                                             vmem_limit_bytes=_vmem_limit(work)),
        name="sample_diff",
    )(page_table.reshape(-1), qbd, *([ck] * g_), *([cv] * g_), knew_p, vnew_p, bnear, bnew, lam, gln)


def _sample_moba(qbd, qbf, page_table, cache_k, cache_v, knew, vnew, bnear, bnew):
    ns, n_pages = page_table.shape
    npool, page, nh, dh = cache_k.shape
    width = nh * dh
    g_ = min(PAGES_PER_STEP, n_pages)
    ck = cache_k.transpose(0, 2, 3, 1).reshape(npool, width, page)
    cv = cache_v.transpose(0, 2, 3, 1).reshape(npool, width, page)
    t = knew.shape[1]
    rows = qbd.shape[1]
    c = g_ * page
    nb = n_pages * page // MOBA_BLOCK
    sq = pl.Squeezed()
    full = lambda a: pl.BlockSpec(a.shape, lambda n, s, pt: (0,) * a.ndim)
    per_seq = lambda a: pl.BlockSpec((sq,) + a.shape[1:], lambda n, s, pt: (n,) + (0,) * (a.ndim - 1))
    knew_p, vnew_p = _pad_new(knew), _pad_new(vnew)
    grid_spec = pltpu.PrefetchScalarGridSpec(
        num_scalar_prefetch=1,
        grid=(ns, n_pages // g_),
        in_specs=[per_seq(qbd), per_seq(qbf)] + _page_specs(g_, n_pages, (width, page))
                 + [per_seq(knew_p), per_seq(vnew_p), full(bnear), full(bnew)],
        out_specs=pl.BlockSpec((sq, t, width), lambda n, s, pt: (n, 0, 0)),
        scratch_shapes=[pltpu.VMEM((width, c), BF16), pltpu.VMEM((width, c), BF16),
                        pltpu.VMEM((rows, LANES), F32), pltpu.VMEM((rows, LANES), F32),
                        pltpu.VMEM((nb, rows, width), F32), pltpu.VMEM((width, LANES), F32)],
    )
    work = (2 * 2 * g_ * page * width * 4 + 2 * c * width * 2 + nb * rows * (width + 2 * LANES) * 4
            + 8 * rows * width * 4)
    return pl.pallas_call(
        functools.partial(_sample_moba_kernel, n_pages_step=g_),
        grid_spec=grid_spec,
        out_shape=jax.ShapeDtypeStruct((ns, t, width), F32),
        compiler_params=pltpu.CompilerParams(dimension_semantics=("arbitrary", "arbitrary"),
                                             vmem_limit_bytes=_vmem_limit(work)),
        name="sample_moba",
    )(page_table.reshape(-1), qbd, qbf, *([ck] * g_), *([cv] * g_), knew_p, vnew_p, bnear, bnew)


def _block_diag_q(q, n_heads):
    ns, t, w = q.shape
    groups = 2 if n_heads == H_A else 1
    n_sl = n_heads * groups
    sl = w // n_sl
    owner = jnp.arange(w) // sl
    rows = jnp.arange(n_sl)
    mask = (owner[None, :] == rows[:, None]).astype(q.dtype)
    out = q[:, None, :, :] * mask[None, :, None, :]
    return out.reshape(ns, n_sl * t, w)


def _sample_bias_tables(tabp, groups, t, near_len):
    n_heads = tabp.shape[0]
    tt = np.arange(t)
    c = np.arange(near_len)
    b_near = _bucket_of_dist(near_len + tt[:, None] - c[None, :])
    tn = np.arange(NEW_PAD)
    d_new = tt[:, None] - tn[None, :]
    b_new = _bucket_of_dist(d_new)
    valid = (d_new >= 0) & (tn[None, :] < t)
    near = tabp[:, b_near]
    new = jnp.where(valid[None], tabp[:, b_new], NEG_INF)
    rep = lambda a: jnp.broadcast_to(a[:, None], (n_heads, groups) + a.shape[1:]).reshape(
        n_heads * groups * t, a.shape[-1])
    return rep(near).astype(F32), rep(new).astype(F32)


def _outproj_kernel(x_ref, y_ref, z_ref, p_ref, wo_ref, wpg_ref, wple_ref, gple_ref, gfin_ref, out_ref, *, final):
    z = z_ref[...]
    yg = (y_ref[...] * (z * jax.nn.sigmoid(z))).astype(BF16)
    h = x_ref[...] + jnp.dot(yg, wo_ref[...], preferred_element_type=F32)
    hn = h * lax.rsqrt(jnp.mean(h * h, axis=-1, keepdims=True) + EPS)
    hn = (hn * gple_ref[...]).astype(BF16)
    gate = jax.nn.sigmoid(jnp.dot(hn, wpg_ref[...], preferred_element_type=F32))
    ple = jnp.dot(p_ref[...].astype(BF16), wple_ref[...], preferred_element_type=F32)
    h = h + gate * ple
    if final:
        h = h * lax.rsqrt(jnp.mean(h * h, axis=-1, keepdims=True) + EPS)
        h = h * gfin_ref[...]
    out_ref[...] = h


def _outproj(x2, y2, z2, p2, wo_b, wpg_b, wple_b, g_ple, g_final, final):
    R, D = x2.shape
    ts = ROW_TILE
    row = lambda w: pl.BlockSpec((ts, w), lambda i: (i, 0))
    full = lambda a: pl.BlockSpec(a.shape, lambda i: (0, 0))
    gp, gf = g_ple.reshape(1, D), g_final.reshape(1, D)
    work = 2 * (ts * (2 * D + 2 * D_MIX + p2.shape[1]) * 4 + (D_MIX * D + D * D + p2.shape[1] * D) * 2)
    return pl.pallas_call(
        functools.partial(_outproj_kernel, final=final),
        grid=(R // ts,),
        in_specs=[row(D), row(D_MIX), row(D_MIX), row(p2.shape[1]), full(wo_b), full(wpg_b), full(wple_b),
                  full(gp), full(gf)],
        out_specs=row(D),
        out_shape=jax.ShapeDtypeStruct((R, D), F32),
        compiler_params=pltpu.CompilerParams(dimension_semantics=("arbitrary",), vmem_limit_bytes=_vmem_limit(work)),
        name="outproj",
    )(x2, y2, z2, p2, wo_b, wpg_b, wple_b, gp, gf)


def kernel(x_prompt, x_sample, cache_diff_k, cache_diff_v, cache_moba_k, cache_moba_v, page_table, p_prompt, p_sample, w_in, w_out, g_mix, lam, subln_g, rel_bias, w_ple, w_pg, g_ple, g_final):
    depth = w_in.shape[0]
    B, S, D = x_prompt.shape
    NS, T, _ = x_sample.shape
    n_pages = page_table.shape[1]
    page = cache_diff_k.shape[2]
    past = n_pages * page
    g_pages = min(PAGES_PER_STEP, n_pages)
    assert S % ATT_TILE == 0 and ATT_TILE == MOBA_BLOCK == ROW_TILE
    assert (B * S) % ROW_TILE == 0 and (NS * T) % ROW_TILE == 0
    assert past % MOBA_BLOCK == 0 and MOBA_BLOCK % page == 0 and T <= page and T <= NEW_PAD
    assert n_pages % g_pages == 0 and (g_pages * page) % MOBA_BLOCK == 0
    assert MOBA_TOPK <= past // MOBA_BLOCK <= LANES and ATT_TILE >= MAX_DIST

    tabp = ((rel_bias - rel_bias[N_BUCKETS - 1:N_BUCKETS, :]) * LOG2E).T.astype(F32)
    tab_a, tab_b = tabp[:H_A], tabp[H_A:]
    near_d, new_d = _sample_bias_tables(tab_a, 2, T, page)
    near_d = jnp.pad(near_d, ((0, 0), (g_pages * page - page, 0)))
    near_m, new_m = _sample_bias_tables(tab_b, 1, T, MOBA_BLOCK)

    hp, hs = x_prompt, x_sample
    outs = [[] for _ in range(8)]
    for layer in range(depth):
        lam_init = 0.8 - 0.6 * math.exp(-0.3 * layer)
        final = layer == depth - 1
        w_in_b = w_in[layer].astype(BF16)
        wo_b, wpg_b, wple_b = w_out[layer].astype(BF16), w_pg[layer].astype(BF16), w_ple[layer].astype(BF16)

        (ka, va, kb, vb, z, qa_s, ka_b, va_b, qb_s, kb_b, vb_b, qb_f, km) = _inproj(
            hp.reshape(B * S, D), g_mix[layer], w_in_b)
        r3 = lambda a: a.reshape(B, S, a.shape[-1])
        ya = _prompt_attn(r3(qa_s), r3(ka_b), r3(va_b), tab_a, moba=False, lam_init=lam_init,
                          lam=lam[layer], subln_g=subln_g[layer])
        yb = _prompt_attn(r3(qb_s), r3(kb_b), r3(vb_b), tab_b, moba=True, lam_init=lam_init,
                          q_f=r3(qb_f), kmean=km.reshape(B, S // MOBA_BLOCK, W_B))
        y = jnp.concatenate([ya, yb], axis=-1).reshape(B * S, D_MIX)
        hp = _outproj(hp.reshape(B * S, D), y, z, p_prompt[layer].reshape(B * S, -1), wo_b, wpg_b, wple_b,
                      g_ple[layer], g_final, final).reshape(B, S, D)
        outs[0].append(ka.reshape(B, S, H_A, 2 * DH_A)); outs[1].append(va.reshape(B, S, H_A, 2 * DH_A))
        outs[2].append(kb.reshape(B, S, H_B, DH_B)); outs[3].append(vb.reshape(B, S, H_B, DH_B))

        (ka, va, kb, vb, z, qa_s, _, _, qb_s, _, _, qb_f, _) = _inproj(hs.reshape(NS * T, D), g_mix[layer], w_in_b)
        s3 = lambda a: a.reshape(NS, T, a.shape[-1])
        ya = _sample_diff(_block_diag_q(s3(qa_s), H_A), page_table, cache_diff_k[layer], cache_diff_v[layer],
                          s3(ka), s3(va), near_d, new_d, lam[layer], subln_g[layer], lam_init)
        yb = _sample_moba(_block_diag_q(s3(qb_s), H_B), _block_diag_q(s3(qb_f), H_B), page_table,
                          cache_moba_k[layer], cache_moba_v[layer], s3(kb), s3(vb), near_m, new_m)
        y = jnp.concatenate([ya, yb], axis=-1).reshape(NS * T, D_MIX)
        hs = _outproj(hs.reshape(NS * T, D), y, z, p_sample[layer].reshape(NS * T, -1), wo_b, wpg_b, wple_b,
                      g_ple[layer], g_final, final).reshape(NS, T, D)
        outs[4].append(ka.reshape(NS, T, H_A, 2 * DH_A)); outs[5].append(va.reshape(NS, T, H_A, 2 * DH_A))
        outs[6].append(kb.reshape(NS, T, H_B, DH_B)); outs[7].append(vb.reshape(NS, T, H_B, DH_B))

    return (hp, hs) + tuple(jnp.stack(o) for o in outs)
```

```python
import functools
import math

import jax
import jax.numpy as jnp
import numpy as np
from jax import lax
from jax.experimental import pallas as pl
from jax.experimental.pallas import tpu as pltpu

F32 = jnp.float32
BF16 = jnp.bfloat16
NEG_INF = float("-inf")
POS_INF = float("inf")

H_A = 4
DH_A = 64
H_B = 8
DH_B = 64
W_A = H_A * 2 * DH_A
W_B = H_B * DH_B
D_MIX = W_A + W_B
MOBA_BLOCK = 256
MOBA_TOPK = 3
N_BUCKETS = 32
MAX_DIST = 128
EPS = 1e-6
LOG2E = math.log2(math.e)
Q_SCALE = DH_A ** -0.5 * LOG2E

LANES = 128
BF16_SUBLANES = 16
V7X_VMEM_BYTES = 64 * 1024 * 1024
COMPILER_SCRATCH_BYTES = 16 * 1024 * 1024

ROW_TILE = 256
ATT_TILE = 256
HEAD_STREAMS = 4
PAGES_PER_STEP = 16
PAGES_PER_DOT = 8
NEW_PAD = BF16_SUBLANES


def _vmem_limit(nbytes):
    return int(min(nbytes + COMPILER_SCRATCH_BYTES, V7X_VMEM_BYTES - 8 * 1024 * 1024))


def _bucket_of_dist(n):
    n = np.maximum(n, 0)
    max_exact = N_BUCKETS // 2
    nf = np.maximum(n, 1).astype(np.float32)
    large = max_exact + (np.log(nf / np.float32(max_exact)) / np.float32(math.log(MAX_DIST / max_exact))
                         * np.float32(N_BUCKETS - max_exact)).astype(np.int32)
    large = np.minimum(large, N_BUCKETS - 1)
    return np.where(n < max_exact, n, large).astype(np.int32)


def _bucket_upper_bounds():
    b = _bucket_of_dist(np.arange(4 * MAX_DIST))
    return [(k, int(np.nonzero(b <= k)[0].max())) for k in range(N_BUCKETS - 1)]


BUCKET_HI = _bucket_upper_bounds()


def _inproj_kernel(x_ref, g_ref, w_ref, ka_ref, va_ref, kb_ref, vb_ref, z_ref,
                   qa_s_ref, ka_b_ref, va_b_ref, qb_s_ref, kb_b_ref, vb_b_ref, qb_f_ref, km_ref):
    x = x_ref[...]
    xn = x * lax.rsqrt(jnp.mean(x * x, axis=-1, keepdims=True) + EPS)
    xn = (xn * g_ref[...]).astype(BF16)

    def proj(c0, c1):
        return jnp.dot(xn, w_ref[:, c0:c1], preferred_element_type=F32)

    qa = proj(0, W_A) * Q_SCALE
    qa_s_ref[...] = qa.astype(BF16)
    ka = proj(W_A, 2 * W_A)
    ka_ref[...] = ka
    ka_b_ref[...] = ka.astype(BF16)
    va = proj(2 * W_A, 3 * W_A)
    va_ref[...] = va
    va_b_ref[...] = va.astype(BF16)
    o = 3 * W_A
    qb = proj(o, o + W_B) * Q_SCALE
    qb_s_ref[...] = qb.astype(BF16)
    qb_f_ref[...] = qb
    kb = proj(o + W_B, o + 2 * W_B)
    kb_ref[...] = kb
    kb_b_ref[...] = kb.astype(BF16)
    km_ref[0] = jnp.mean(kb, axis=0, keepdims=True)
    vb = proj(o + 2 * W_B, o + 3 * W_B)
    vb_ref[...] = vb
    vb_b_ref[...] = vb.astype(BF16)
    z_ref[...] = proj(o + 3 * W_B, o + 3 * W_B + D_MIX)


def _inproj(x2, g_mix, w_in_b):
    R, D = x2.shape
    d_in = w_in_b.shape[1]
    ts = ROW_TILE
    nt = R // ts
    row = lambda w: pl.BlockSpec((ts, w), lambda i: (i, 0))
    f32o = lambda w: jax.ShapeDtypeStruct((R, w), F32)
    b16o = lambda w: jax.ShapeDtypeStruct((R, w), BF16)
    out_shape = (f32o(W_A), f32o(W_A), f32o(W_B), f32o(W_B), f32o(D_MIX),
                 b16o(W_A), b16o(W_A), b16o(W_A), b16o(W_B), b16o(W_B), b16o(W_B), f32o(W_B),
                 jax.ShapeDtypeStruct((nt, 1, W_B), F32))
    out_specs = (row(W_A), row(W_A), row(W_B), row(W_B), row(D_MIX),
                 row(W_A), row(W_A), row(W_A), row(W_B), row(W_B), row(W_B), row(W_B),
                 pl.BlockSpec((1, 1, W_B), lambda i: (i, 0, 0)))
    work = 2 * (ts * D * 4 + D * d_in * 2 + ts * (4 * 512 + D_MIX + 512) * 4 + ts * 6 * 512 * 2)
    return pl.pallas_call(
        _inproj_kernel,
        grid=(nt,),
        in_specs=[row(D), pl.BlockSpec((1, D), lambda i: (0, 0)), pl.BlockSpec((D, d_in), lambda i: (0, 0))],
        out_specs=out_specs,
        out_shape=out_shape,
        compiler_params=pltpu.CompilerParams(dimension_semantics=("arbitrary",), vmem_limit_bytes=_vmem_limit(work)),
        name="inproj",
    )(x2, g_mix.reshape(1, D), w_in_b)


def _bias_of_dist(dist, tab_ref, h):
    v = jnp.full(dist.shape, tab_ref[h, N_BUCKETS - 1], F32)
    for b, hi in reversed(BUCKET_HI):
        v = jnp.where(dist <= hi, tab_ref[h, b], v)
    return v


def _softmax_tile(s, vt, m_ref, acc_ref, colsel):
    m_prev = m_ref[...]
    m_tile = jnp.max(s, axis=0, keepdims=True)
    if colsel is not None:
        m_tile = jnp.where(colsel, m_tile, NEG_INF)
    m_new = jnp.maximum(m_prev, m_tile)
    m_safe = jnp.where(m_new == NEG_INF, 0.0, m_new)
    alpha = jnp.exp2(m_prev - m_safe)
    sub = m_safe if colsel is None else jnp.where(colsel, m_safe, POS_INF)
    p = jnp.exp2(s - sub).astype(BF16)
    acc_ref[...] = acc_ref[...] * alpha + jnp.dot(vt, p, preferred_element_type=F32)
    m_ref[...] = m_new


def _prompt_attn_kernel(*refs, moba, lam_init):
    if moba:
        (tab_ref, qt_ref, k_ref, vt_ref, qf_ref, km_ref, out_ref, m_ref, acc_ref, bias_ref, sel_ref) = refs
    else:
        (tab_ref, qt_ref, k_ref, vt_ref, lam_ref, g_ref, out_ref, m_ref, acc_ref, bias_ref) = refs
    hg = pl.program_id(1)
    i = pl.program_id(2)
    ns = qt_ref.shape[0]
    tq = qt_ref.shape[-1]
    tk = k_ref.shape[-2]
    half = qt_ref.shape[1] // 2

    @pl.when(i == 0)
    def _():
        kk = lax.broadcasted_iota(jnp.int32, (tk, tq), 0)
        qq = lax.broadcasted_iota(jnp.int32, (tk, tq), 1)
        for s in range(ns):
            hh = hg * ns + s
            heads = (2 * hh, 2 * hh + 1) if moba else (hh, hh)
            for c, h in enumerate(heads):
                if c == 1 and heads[1] is heads[0]:
                    bias_ref[s, 0, :, tq:] = bias_ref[s, 0, :, :tq]
                    bias_ref[s, 1, :, tq:] = bias_ref[s, 1, :, :tq]
                    continue
                bias_ref[s, 0, :, c * tq:(c + 1) * tq] = _bias_of_dist(qq + tk - kk, tab_ref, h)
                d = qq - kk
                bias_ref[s, 1, :, c * tq:(c + 1) * tq] = jnp.where(d >= 0, _bias_of_dist(d, tab_ref, h), NEG_INF)

    q2 = []
    for s in range(ns):
        qt = qt_ref[s]
        feat = lax.broadcasted_iota(jnp.int32, qt.shape, 0)
        zero = jnp.zeros_like(qt)
        q2.append(jnp.concatenate([jnp.where(feat < half, qt, zero), jnp.where(feat >= half, qt, zero)], axis=1))

    m_ref[...] = jnp.full(m_ref.shape, NEG_INF, F32)
    acc_ref[...] = jnp.zeros(acc_ref.shape, F32)

    if moba:
        nb = km_ref.shape[1]
        hi = lax.Precision.HIGHEST
        for s in range(ns):
            km = km_ref[s]
            col = lax.broadcasted_iota(jnp.int32, km.shape, 1)
            qf = qf_ref[s]
            g = jnp.concatenate(
                [jnp.dot(jnp.where(col < half, km, 0.0), qf, precision=hi, preferred_element_type=F32),
                 jnp.dot(jnp.where(col >= half, km, 0.0), qf, precision=hi, preferred_element_type=F32)], axis=1)
            blk = lax.broadcasted_iota(jnp.int32, g.shape, 0)
            past = blk < i
            g = jnp.where(past, g, NEG_INF)
            sel = jnp.zeros(g.shape, F32)
            for _ in range(min(MOBA_TOPK, nb - 1)):
                mx = jnp.max(g, axis=0, keepdims=True)
                idx = jnp.min(jnp.where(g == mx, blk, nb), axis=0, keepdims=True)
                hit = blk == idx
                sel = jnp.where(jnp.logical_and(hit, past), 1.0, sel)
                g = jnp.where(hit, NEG_INF, g)
            for r in range(nb):
                sel_ref[s, r] = sel[r:r + 1, :]

    def update(j, bias_idx, masked):
        scores = [jnp.dot(k_ref[s, j], q2[s], preferred_element_type=F32) for s in range(ns)]
        for s in range(ns):
            sc = scores[s] if bias_idx is None else scores[s] + bias_ref[s, bias_idx]
            colsel = (sel_ref[s, j] > 0.5) if (moba and masked) else None
            _softmax_tile(sc, vt_ref[s, j], m_ref.at[s], acc_ref.at[s], colsel)

    def far_body(j, carry):
        update(j, None, True)
        return carry

    lax.fori_loop(0, jnp.maximum(i - 1, 0), far_body, 0)

    @pl.when(i >= 1)
    def _():
        update(i - 1, 0, True)

    update(i, 1, False)

    if not moba:
        lam = lam_ref[...]
        lam_l = (jnp.exp(jnp.sum(lam[0:1] * lam[1:2], axis=-1, keepdims=True))
                 - jnp.exp(jnp.sum(lam[2:3] * lam[3:4], axis=-1, keepdims=True)) + lam_init)
    feat_w = 2 * half
    for s in range(ns):
        acc = acc_ref[s]
        inv = 1.0 / acc[feat_w:feat_w + 1, :]
        if moba:
            o = jnp.concatenate([acc[:half, :tq] * inv[:, :tq], acc[half:feat_w, tq:] * inv[:, tq:]], axis=0)
        else:
            o = acc[:feat_w, :tq] * inv[:, :tq] - lam_l * (acc[:feat_w, tq:] * inv[:, tq:])
            o = o * lax.rsqrt(jnp.mean(o * o, axis=0, keepdims=True) + EPS)
            o = (o * g_ref[...]) * (1.0 - lam_init)
        out_ref[:, s * feat_w:(s + 1) * feat_w] = o.T


def _tiles_major(a, nt, t, transpose):
    b = a.shape[0]
    a = a.reshape(b, nt, t, 4, LANES)
    return a.transpose(0, 3, 1, 4, 2) if transpose else a.transpose(0, 3, 1, 2, 4)


def _prompt_attn(q_s, k_b, v_b, tab, *, moba, lam_init, lam=None, subln_g=None, q_f=None, kmean=None):
    B, S, _ = q_s.shape
    t = ATT_TILE
    nt = S // t
    ns = HEAD_STREAMS
    qt = _tiles_major(q_s, nt, t, True)
    kt = _tiles_major(k_b, nt, t, False)
    vt = _tiles_major(v_b, nt, t, True)
    ones = jnp.zeros((B, 4, nt, BF16_SUBLANES, t), BF16).at[:, :, :, 0, :].set(1.0)
    vt = jnp.concatenate([vt, ones], axis=3)
    vrows = LANES + BF16_SUBLANES
    sq = pl.Squeezed()
    in_specs = [
        pl.BlockSpec(memory_space=pltpu.SMEM),
        pl.BlockSpec((sq, ns, sq, LANES, t), lambda b, h, i: (b, h, i, 0, 0)),
        pl.BlockSpec((sq, ns, nt, t, LANES), lambda b, h, i: (b, h, 0, 0, 0), pipeline_mode=pl.Buffered(1)),
        pl.BlockSpec((sq, ns, nt, vrows, t), lambda b, h, i: (b, h, 0, 0, 0), pipeline_mode=pl.Buffered(1)),
    ]
    scratch = [pltpu.VMEM((ns, 1, 2 * t), F32), pltpu.VMEM((ns, vrows, 2 * t), F32),
               pltpu.VMEM((ns, 2, t, 2 * t), F32)]
    if moba:
        nb = kmean.shape[1]
        kmh = kmean.reshape(B, nb, 4, LANES).transpose(0, 2, 1, 3)
        args = (tab, qt, kt, vt, _tiles_major(q_f, nt, t, True), kmh)
        in_specs += [pl.BlockSpec((sq, ns, sq, LANES, t), lambda b, h, i: (b, h, i, 0, 0)),
                     pl.BlockSpec((sq, ns, nb, LANES), lambda b, h, i: (b, h, 0, 0))]
        scratch += [pltpu.VMEM((ns, nb, 1, 2 * t), F32)]
    else:
        args = (tab, qt, kt, vt, lam, subln_g.reshape(LANES, 1))
        in_specs += [pl.BlockSpec(lam.shape, lambda b, h, i: (0, 0)),
                     pl.BlockSpec((LANES, 1), lambda b, h, i: (0, 0))]
    work = (ns * (S * LANES * 2 + S * vrows * 2) + 2 * ns * (2 * t * LANES * 6 + t * LANES * 4)
            + ns * (2 * t * 2 * t * 4 + vrows * 2 * t * 4 + 32 * 8 * 2 * t * 4) + 8 * t * 2 * t * 4)
    return pl.pallas_call(
        functools.partial(_prompt_attn_kernel, moba=moba, lam_init=lam_init),
        grid=(B, 4 // ns, nt),
        in_specs=in_specs,
        out_specs=pl.BlockSpec((sq, t, ns * LANES), lambda b, h, i: (b, i, h)),
        out_shape=jax.ShapeDtypeStruct((B, S, 4 * LANES), F32),
        scratch_shapes=scratch,
        compiler_params=pltpu.CompilerParams(dimension_semantics=("arbitrary", "arbitrary", "arbitrary"),
                                             vmem_limit_bytes=_vmem_limit(work)),
        name="prompt_moba" if moba else "prompt_diff",
    )(*args)


def _nt_dot(a, b):
    return lax.dot_general(a, b, (((1,), (1,)), ((), ())), preferred_element_type=F32)


def _row_softmax_tile(s, v, m_ref, l_ref, acc_ref):
    m_prev = m_ref[...]
    m_new = jnp.maximum(m_prev, jnp.max(s, axis=-1, keepdims=True))
    alpha = jnp.exp2(m_prev - m_new)
    p = jnp.exp2(s - m_new)
    l_ref[...] = alpha * l_ref[...] + jnp.sum(p, axis=-1, keepdims=True)
    acc_ref[...] = acc_ref[...] * alpha + jnp.dot(p.astype(BF16), v, preferred_element_type=F32)
    m_ref[...] = m_new


def _sample_diff_kernel(pt_ref, qbd_ref, *rest, n_pages_step, lam_init):
    g_ = n_pages_step
    k_pages = rest[:g_]
    v_pages = rest[g_:2 * g_]
    (knew_ref, vnew_ref, bnear_ref, bnew_ref, lam_ref, gln_ref, out_ref,
     kbuf, vbuf, m_ref, l_ref, acc_ref) = rest[2 * g_:]
    step = pl.program_id(1)
    last = pl.num_programs(1) - 1
    page = k_pages[0].shape[0] // H_A

    @pl.when(step == 0)
    def _():
        m_ref[...] = jnp.full(m_ref.shape, NEG_INF, F32)
        l_ref[...] = jnp.zeros(l_ref.shape, F32)
        acc_ref[...] = jnp.zeros(acc_ref.shape, F32)

    def regroup(pages, buf, p):
        for h in range(H_A):
            rows = pl.ds(h, page, stride=H_A)
            buf[p * page:(p + 1) * page, h * LANES:(h + 1) * LANES] = pages[p][rows, :].astype(BF16)

    qbd = qbd_ref[...]
    cw = bnear_ref.shape[1]
    ppd = cw // page
    scores = []
    for c in range(g_ // ppd):
        for p in range(c * ppd, (c + 1) * ppd):
            regroup(k_pages, kbuf, p)
        scores.append(_nt_dot(qbd, kbuf[c * cw:(c + 1) * cw, :]))
    for p in range(g_):
        regroup(v_pages, vbuf, p)
    scores[-1] = scores[-1] + bnear_ref[...] * (step == last).astype(F32)
    m_prev = m_ref[...]
    m_new = m_prev
    for sc in scores:
        m_new = jnp.maximum(m_new, jnp.max(sc, axis=-1, keepdims=True))
    alpha = jnp.exp2(m_prev - m_new)
    probs = [jnp.exp2(sc - m_new) for sc in scores]
    l_new = alpha * l_ref[...]
    acc = acc_ref[...] * alpha
    for c, p in enumerate(probs):
        l_new = l_new + jnp.sum(p, axis=-1, keepdims=True)
        acc = acc + jnp.dot(p.astype(BF16), vbuf[c * cw:(c + 1) * cw, :], preferred_element_type=F32)
    l_ref[...] = l_new
    acc_ref[...] = acc
    m_ref[...] = m_new

    @pl.when(step == last)
    def _():
        s_new = _nt_dot(qbd, knew_ref[...]) + bnew_ref[...]
        _row_softmax_tile(s_new, vnew_ref[...], m_ref, l_ref, acc_ref)
        acc = acc_ref[...]
        inv = 1.0 / l_ref[...]
        lam = lam_ref[...]
        lam_l = (jnp.exp(jnp.sum(lam[0:1] * lam[1:2], axis=-1, keepdims=True))
                 - jnp.exp(jnp.sum(lam[2:3] * lam[3:4], axis=-1, keepdims=True)) + lam_init)
        t = out_ref.shape[0]
        rows = 2 * t
        for h in range(H_A):
            blk = acc[h * rows:(h + 1) * rows, h * LANES:(h + 1) * LANES] * inv[h * rows:(h + 1) * rows]
            o = blk[:t] - lam_l * blk[t:]
            o = o * lax.rsqrt(jnp.mean(o * o, axis=-1, keepdims=True) + EPS)
            out_ref[:, h * LANES:(h + 1) * LANES] = (o * gln_ref[...]) * (1.0 - lam_init)


def _sample_moba_kernel(pt_ref, qbd_ref, qbf_ref, *rest, n_pages_step):
    g_ = n_pages_step
    k_pages = rest[:g_]
    v_pages = rest[g_:2 * g_]
    (knew_ref, vnew_ref, bnear_ref, bnew_ref, out_ref,
     kbuf, vbuf, pm_ref, pl_ref, po_ref, km_ref) = rest[2 * g_:]
    step = pl.program_id(1)
    last = pl.num_programs(1) - 1
    page = k_pages[0].shape[1]
    ppb = MOBA_BLOCK // page
    bps = g_ // ppb
    nb = po_ref.shape[0]

    @pl.when(step == 0)
    def _():
        km_ref[...] = jnp.zeros(km_ref.shape, F32)
        pm_ref[...] = jnp.full(pm_ref.shape, NEG_INF, F32)
        pl_ref[...] = jnp.zeros(pl_ref.shape, F32)

    qbd = qbd_ref[...]
    cw = PAGES_PER_DOT * page if g_ % PAGES_PER_DOT == 0 else g_ * page
    ppd = cw // page
    scores = []
    for c in range(g_ // ppd):
        for p in range(c * ppd, (c + 1) * ppd):
            kbuf[:, p * page:(p + 1) * page] = k_pages[p][...].astype(BF16)
        scores.append(jnp.dot(qbd, kbuf[:, c * cw:(c + 1) * cw], preferred_element_type=F32))
    for p in range(g_):
        vbuf[:, p * page:(p + 1) * page] = v_pages[p][...].astype(BF16)

    near = (step == last).astype(F32)
    km_lane = lax.broadcasted_iota(jnp.int32, km_ref.shape, 1)
    stat_lane = lax.broadcasted_iota(jnp.int32, pm_ref.shape, 1)
    km_new, pm_new, pl_new = km_ref[...], pm_ref[...], pl_ref[...]
    probs = []
    for jb in range(bps):
        j = step * bps + jb
        ksum = k_pages[jb * ppb][...]
        for q in range(1, ppb):
            ksum = ksum + k_pages[jb * ppb + q][...]
        kmean = jnp.sum(ksum, axis=-1, keepdims=True) * (1.0 / MOBA_BLOCK)
        km_new = jnp.where(km_lane == j, kmean, km_new)
        off = jb * MOBA_BLOCK - (jb * MOBA_BLOCK // cw) * cw
        s = scores[jb * MOBA_BLOCK // cw][:, off:off + MOBA_BLOCK]
        if jb == bps - 1:
            s = s + bnear_ref[...] * near
        m = jnp.max(s, axis=-1, keepdims=True)
        p = jnp.exp2(s - m)
        pm_new = jnp.where(stat_lane == j, m, pm_new)
        pl_new = jnp.where(stat_lane == j, jnp.sum(p, axis=-1, keepdims=True), pl_new)
        probs.append(p.astype(BF16))
    km_ref[...] = km_new
    pm_ref[...] = pm_new
    pl_ref[...] = pl_new
    for jb in range(bps):
        po_ref[step * bps + jb] = _nt_dot(probs[jb], vbuf[:, jb * MOBA_BLOCK:(jb + 1) * MOBA_BLOCK])

    @pl.when(step == last)
    def _():
        gate = jnp.dot(qbf_ref[...], km_ref[...], precision=lax.Precision.HIGHEST, preferred_element_type=F32)
        blk = lax.broadcasted_iota(jnp.int32, gate.shape, 1)
        gate = jnp.where(blk < nb, gate, NEG_INF)
        sel = jnp.zeros(gate.shape, F32)
        for _ in range(min(MOBA_TOPK, nb)):
            mx = jnp.max(gate, axis=-1, keepdims=True)
            idx = jnp.min(jnp.where(gate == mx, blk, LANES), axis=-1, keepdims=True)
            hit = blk == idx
            sel = jnp.where(hit, 1.0, sel)
            gate = jnp.where(hit, NEG_INF, gate)
        s_own = _nt_dot(qbd, knew_ref[...]) + bnew_ref[...]
        m_own = jnp.max(s_own, axis=-1, keepdims=True)
        p_own = jnp.exp2(s_own - m_own)
        chosen = sel > 0.5
        m = jnp.maximum(m_own, jnp.max(jnp.where(chosen, pm_ref[...], NEG_INF), axis=-1, keepdims=True))
        w = jnp.where(chosen, jnp.exp2(pm_ref[...] - m), 0.0)
        a_own = jnp.exp2(m_own - m)
        l = a_own * jnp.sum(p_own, axis=-1, keepdims=True) + jnp.sum(w * pl_ref[...], axis=-1, keepdims=True)
        o = a_own * jnp.dot(p_own.astype(BF16), vnew_ref[...], preferred_element_type=F32)
        for j in range(nb):
            o = o + w[:, j:j + 1] * po_ref[j]
        o = o * (1.0 / l)
        t = out_ref.shape[0]
        lane_head = lax.broadcasted_iota(jnp.int32, (t, W_B), 1) // DH_B
        y = jnp.zeros((t, W_B), F32)
        for h in range(H_B):
            y = jnp.where(lane_head == h, o[h * t:(h + 1) * t, :], y)
        out_ref[...] = y


def _page_specs(g_, n_pages, block):
    def spec(p):
        return pl.BlockSpec((pl.Squeezed(),) + block, lambda n, s, pt: (pt[n * n_pages + s * g_ + p], 0, 0))
    return [spec(p) for p in range(g_)] + [spec(p) for p in range(g_)]


def _pad_new(a):
    ns, t, w = a.shape
    return jnp.pad(a, ((0, 0), (0, NEW_PAD - t), (0, 0))).astype(BF16)


def _sample_diff(qbd, page_table, cache_k, cache_v, knew, vnew, bnear, bnew, lam, subln_g, lam_init):
    ns, n_pages = page_table.shape
    npool, page, nh, dh = cache_k.shape
    width = nh * dh
    g_ = min(PAGES_PER_STEP, n_pages)
    ck = cache_k.reshape(npool, page * nh, dh)
    cv = cache_v.reshape(npool, page * nh, dh)
    t = knew.shape[1]
    rows = qbd.shape[1]
    c = g_ * page
    sq = pl.Squeezed()
    full = lambda a: pl.BlockSpec(a.shape, lambda n, s, pt: (0,) * a.ndim)
    per_seq = lambda a: pl.BlockSpec((sq,) + a.shape[1:], lambda n, s, pt: (n,) + (0,) * (a.ndim - 1))
    knew_p, vnew_p = _pad_new(knew), _pad_new(vnew)
    gln = subln_g.reshape(1, LANES)
    grid_spec = pltpu.PrefetchScalarGridSpec(
        num_scalar_prefetch=1,
        grid=(ns, n_pages // g_),
        in_specs=[per_seq(qbd)] + _page_specs(g_, n_pages, (page * nh, dh))
                 + [per_seq(knew_p), per_seq(vnew_p), full(bnear), full(bnew), full(lam), full(gln)],
        out_specs=pl.BlockSpec((sq, t, width), lambda n, s, pt: (n, 0, 0)),
        scratch_shapes=[pltpu.VMEM((c, width), BF16), pltpu.VMEM((c, width), BF16),
                        pltpu.VMEM((rows, 1), F32), pltpu.VMEM((rows, 1), F32), pltpu.VMEM((rows, width), F32)],
    )
    work = 2 * 2 * g_ * page * width * 4 + 2 * c * width * 2 + 4 * rows * c * 4
    return pl.pallas_call(
        functools.partial(_sample_diff_kernel, n_pages_step=g_, lam_init=lam_init),
        grid_spec=grid_spec,
        out_shape=jax.ShapeDtypeStruct((ns, t, width), F32),
        compiler_params=pltpu.CompilerParams(dimension_semantics=("arbitrary", "arbitrary"),
                                             vmem_limit_bytes=_vmem_limit(work)),
        name="sample_diff",
    )(page_table.reshape(-1), qbd, *([ck] * g_), *([cv] * g_), knew_p, vnew_p, bnear, bnew, lam, gln)


def _sample_moba(qbd, qbf, page_table, cache_k, cache_v, knew, vnew, bnear, bnew):
    ns, n_pages = page_table.shape
    npool, page, nh, dh = cache_k.shape
    width = nh * dh
    g_ = min(PAGES_PER_STEP, n_pages)
    ck = cache_k.transpose(0, 2, 3, 1).reshape(npool, width, page)
    cv = cache_v.transpose(0, 2, 3, 1).reshape(npool, width, page)
    t = knew.shape[1]
    rows = qbd.shape[1]
    c = g_ * page
    nb = n_pages * page // MOBA_BLOCK
    sq = pl.Squeezed()
    full = lambda a: pl.BlockSpec(a.shape, lambda n, s, pt: (0,) * a.ndim)
    per_seq = lambda a: pl.BlockSpec((sq,) + a.shape[1:], lambda n, s, pt: (n,) + (0,) * (a.ndim - 1))
    knew_p, vnew_p = _pad_new(knew), _pad_new(vnew)
    grid_spec = pltpu.PrefetchScalarGridSpec(
        num_scalar_prefetch=1,
        grid=(ns, n_pages // g_),
        in_specs=[per_seq(qbd), per_seq(qbf)] + _page_specs(g_, n_pages, (width, page))
                 + [per_seq(knew_p), per_seq(vnew_p), full(bnear), full(bnew)],
        out_specs=pl.BlockSpec((sq, t, width), lambda n, s, pt: (n, 0, 0)),
        scratch_shapes=[pltpu.VMEM((width, c), BF16), pltpu.VMEM((width, c), BF16),
                        pltpu.VMEM((rows, LANES), F32), pltpu.VMEM((rows, LANES), F32),
                        pltpu.VMEM((nb, rows, width), F32), pltpu.VMEM((width, LANES), F32)],
    )
    work = (2 * 2 * g_ * page * width * 4 + 2 * c * width * 2 + nb * rows * (width + 2 * LANES) * 4
            + 8 * rows * width * 4)
    return pl.pallas_call(
        functools.partial(_sample_moba_kernel, n_pages_step=g_),
        grid_spec=grid_spec,
        out_shape=jax.ShapeDtypeStruct((ns, t, width), F32),
        compiler_params=pltpu.CompilerParams(dimension_semantics=("arbitrary", "arbitrary"),
                                             vmem_limit_bytes=_vmem_limit(work)),
        name="sample_moba",
    )(page_table.reshape(-1), qbd, qbf, *([ck] * g_), *([cv] * g_), knew_p, vnew_p, bnear, bnew)


def _block_diag_q(q, n_heads):
    ns, t, w = q.shape
    groups = 2 if n_heads == H_A else 1
    n_sl = n_heads * groups
    sl = w // n_sl
    owner = jnp.arange(w) // sl
    rows = jnp.arange(n_sl)
    mask = (owner[None, :] == rows[:, None]).astype(q.dtype)
    out = q[:, None, :, :] * mask[None, :, None, :]
    return out.reshape(ns, n_sl * t, w)


def _sample_bias_tables(tabp, groups, t, near_len):
    n_heads = tabp.shape[0]
    tt = np.arange(t)
    c = np.arange(near_len)
    b_near = _bucket_of_dist(near_len + tt[:, None] - c[None, :])
    tn = np.arange(NEW_PAD)
    d_new = tt[:, None] - tn[None, :]
    b_new = _bucket_of_dist(d_new)
    valid = (d_new >= 0) & (tn[None, :] < t)
    near = tabp[:, b_near]
    new = jnp.where(valid[None], tabp[:, b_new], NEG_INF)
    rep = lambda a: jnp.broadcast_to(a[:, None], (n_heads, groups) + a.shape[1:]).reshape(
        n_heads * groups * t, a.shape[-1])
    return rep(near).astype(F32), rep(new).astype(F32)


def _outproj_kernel(x_ref, y_ref, z_ref, p_ref, wo_ref, wpg_ref, wple_ref, gple_ref, gfin_ref, out_ref, *, final):
    z = z_ref[...]
    yg = (y_ref[...] * (z * jax.nn.sigmoid(z))).astype(BF16)
    h = x_ref[...] + jnp.dot(yg, wo_ref[...], preferred_element_type=F32)
    hn = h * lax.rsqrt(jnp.mean(h * h, axis=-1, keepdims=True) + EPS)
    hn = (hn * gple_ref[...]).astype(BF16)
    gate = jax.nn.sigmoid(jnp.dot(hn, wpg_ref[...], preferred_element_type=F32))
    ple = jnp.dot(p_ref[...].astype(BF16), wple_ref[...], preferred_element_type=F32)
    h = h + gate * ple
    if final:
        h = h * lax.rsqrt(jnp.mean(h * h, axis=-1, keepdims=True) + EPS)
        h = h * gfin_ref[...]
    out_ref[...] = h


def _outproj(x2, y2, z2, p2, wo_b, wpg_b, wple_b, g_ple, g_final, final):
    R, D = x2.shape
    ts = ROW_TILE
    row = lambda w: pl.BlockSpec((ts, w), lambda i: (i, 0))
    full = lambda a: pl.BlockSpec(a.shape, lambda i: (0, 0))
    gp, gf = g_ple.reshape(1, D), g_final.reshape(1, D)
    work = 2 * (ts * (2 * D + 2 * D_MIX + p2.shape[1]) * 4 + (D_MIX * D + D * D + p2.shape[1] * D) * 2)
    return pl.pallas_call(
        functools.partial(_outproj_kernel, final=final),
        grid=(R // ts,),
        in_specs=[row(D), row(D_MIX), row(D_MIX), row(p2.shape[1]), full(wo_b), full(wpg_b), full(wple_b),
                  full(gp), full(gf)],
        out_specs=row(D),
        out_shape=jax.ShapeDtypeStruct((R, D), F32),
        compiler_params=pltpu.CompilerParams(dimension_semantics=("arbitrary",), vmem_limit_bytes=_vmem_limit(work)),
        name="outproj",
    )(x2, y2, z2, p2, wo_b, wpg_b, wple_b, gp, gf)


def kernel(x_prompt, x_sample, cache_diff_k, cache_diff_v, cache_moba_k, cache_moba_v, page_table, p_prompt, p_sample, w_in, w_out, g_mix, lam, subln_g, rel_bias, w_ple, w_pg, g_ple, g_final):
    depth = w_in.shape[0]
    B, S, D = x_prompt.shape
    NS, T, _ = x_sample.shape
    n_pages = page_table.shape[1]
    page = cache_diff_k.shape[2]
    past = n_pages * page
    g_pages = min(PAGES_PER_STEP, n_pages)
    assert S % ATT_TILE == 0 and ATT_TILE == MOBA_BLOCK == ROW_TILE
    assert (B * S) % ROW_TILE == 0 and (NS * T) % ROW_TILE == 0
    assert past % MOBA_BLOCK == 0 and MOBA_BLOCK % page == 0 and T <= page and T <= NEW_PAD
    assert n_pages % g_pages == 0 and (g_pages * page) % MOBA_BLOCK == 0 and (PAGES_PER_DOT * page) % MOBA_BLOCK == 0
    assert MOBA_TOPK <= past // MOBA_BLOCK <= LANES and ATT_TILE >= MAX_DIST

    tabp = ((rel_bias - rel_bias[N_BUCKETS - 1:N_BUCKETS, :]) * LOG2E).T.astype(F32)
    tab_a, tab_b = tabp[:H_A], tabp[H_A:]
    near_d, new_d = _sample_bias_tables(tab_a, 2, T, page)
    dot_pages = PAGES_PER_DOT if g_pages % PAGES_PER_DOT == 0 else g_pages
    near_d = jnp.pad(near_d, ((0, 0), (dot_pages * page - page, 0)))
    near_m, new_m = _sample_bias_tables(tab_b, 1, T, MOBA_BLOCK)

    hp, hs = x_prompt, x_sample
    outs = [[] for _ in range(8)]
    for layer in range(depth):
        lam_init = 0.8 - 0.6 * math.exp(-0.3 * layer)
        final = layer == depth - 1
        w_in_b = w_in[layer].astype(BF16)
        wo_b, wpg_b, wple_b = w_out[layer].astype(BF16), w_pg[layer].astype(BF16), w_ple[layer].astype(BF16)

        (ka, va, kb, vb, z, qa_s, ka_b, va_b, qb_s, kb_b, vb_b, qb_f, km) = _inproj(
            hp.reshape(B * S, D), g_mix[layer], w_in_b)
        r3 = lambda a: a.reshape(B, S, a.shape[-1])
        ya = _prompt_attn(r3(qa_s), r3(ka_b), r3(va_b), tab_a, moba=False, lam_init=lam_init,
                          lam=lam[layer], subln_g=subln_g[layer])
        yb = _prompt_attn(r3(qb_s), r3(kb_b), r3(vb_b), tab_b, moba=True, lam_init=lam_init,
                          q_f=r3(qb_f), kmean=km.reshape(B, S // MOBA_BLOCK, W_B))
        y = jnp.concatenate([ya, yb], axis=-1).reshape(B * S, D_MIX)
        hp = _outproj(hp.reshape(B * S, D), y, z, p_prompt[layer].reshape(B * S, -1), wo_b, wpg_b, wple_b,
                      g_ple[layer], g_final, final).reshape(B, S, D)
        outs[0].append(ka.reshape(B, S, H_A, 2 * DH_A)); outs[1].append(va.reshape(B, S, H_A, 2 * DH_A))
        outs[2].append(kb.reshape(B, S, H_B, DH_B)); outs[3].append(vb.reshape(B, S, H_B, DH_B))

        (ka, va, kb, vb, z, qa_s, _, _, qb_s, _, _, qb_f, _) = _inproj(hs.reshape(NS * T, D), g_mix[layer], w_in_b)
        s3 = lambda a: a.reshape(NS, T, a.shape[-1])
        ya = _sample_diff(_block_diag_q(s3(qa_s), H_A), page_table, cache_diff_k[layer], cache_diff_v[layer],
                          s3(ka), s3(va), near_d, new_d, lam[layer], subln_g[layer], lam_init)
        yb = _sample_moba(_block_diag_q(s3(qb_s), H_B), _block_diag_q(s3(qb_f), H_B), page_table,
                          cache_moba_k[layer], cache_moba_v[layer], s3(kb), s3(vb), near_m, new_m)
        y = jnp.concatenate([ya, yb], axis=-1).reshape(NS * T, D_MIX)
        hs = _outproj(hs.reshape(NS * T, D), y, z, p_sample[layer].reshape(NS * T, -1), wo_b, wpg_b, wple_b,
                      g_ple[layer], g_final, final).reshape(NS, T, D)
        outs[4].append(ka.reshape(NS, T, H_A, 2 * DH_A)); outs[5].append(va.reshape(NS, T, H_A, 2 * DH_A))
        outs[6].append(kb.reshape(NS, T, H_B, DH_B)); outs[7].append(vb.reshape(NS, T, H_B, DH_B))

    return (hp, hs) + tuple(jnp.stack(o) for o in outs)
```

```python
import functools
import math

import jax
import jax.numpy as jnp
import numpy as np
from jax import lax
from jax.experimental import pallas as pl
from jax.experimental.pallas import tpu as pltpu

F32 = jnp.float32
BF16 = jnp.bfloat16
NEG_INF = float("-inf")
POS_INF = float("inf")

H_A = 4
DH_A = 64
H_B = 8
DH_B = 64
W_A = H_A * 2 * DH_A
W_B = H_B * DH_B
D_MIX = W_A + W_B
MOBA_BLOCK = 256
MOBA_TOPK = 3
N_BUCKETS = 32
MAX_DIST = 128
EPS = 1e-6
LOG2E = math.log2(math.e)
Q_SCALE = DH_A ** -0.5 * LOG2E

LANES = 128
BF16_SUBLANES = 16
V7X_VMEM_BYTES = 64 * 1024 * 1024
COMPILER_SCRATCH_BYTES = 16 * 1024 * 1024

ROW_TILE = 256
ATT_TILE = 256
N_STREAMS = 4
PAGES_PER_STEP = 32
PAGES_PER_DOT = 8
NEW_PAD = BF16_SUBLANES


def _vmem_limit(nbytes):
    return int(min(nbytes + COMPILER_SCRATCH_BYTES, V7X_VMEM_BYTES - 8 * 1024 * 1024))


def _bucket_of_dist(n):
    n = np.maximum(n, 0)
    max_exact = N_BUCKETS // 2
    nf = np.maximum(n, 1).astype(np.float32)
    large = max_exact + (np.log(nf / np.float32(max_exact)) / np.float32(math.log(MAX_DIST / max_exact))
                         * np.float32(N_BUCKETS - max_exact)).astype(np.int32)
    large = np.minimum(large, N_BUCKETS - 1)
    return np.where(n < max_exact, n, large).astype(np.int32)


def _bucket_upper_bounds():
    b = _bucket_of_dist(np.arange(4 * MAX_DIST))
    return [(k, int(np.nonzero(b <= k)[0].max())) for k in range(N_BUCKETS - 1)]


BUCKET_HI = _bucket_upper_bounds()


def _inproj_kernel(x_ref, g_ref, w_ref, ka_ref, va_ref, kb_ref, vb_ref, z_ref,
                   qa_ref, qb_ref, ka_b_ref, va_b_ref, kb_b_ref, vb_b_ref, km_ref):
    x = x_ref[...]
    xn = x * lax.rsqrt(jnp.mean(x * x, axis=-1, keepdims=True) + EPS)
    xn = (xn * g_ref[...]).astype(BF16)

    def proj(c0, c1):
        return jnp.dot(xn, w_ref[:, c0:c1], preferred_element_type=F32)

    qa_ref[...] = proj(0, W_A) * Q_SCALE
    ka = proj(W_A, 2 * W_A)
    ka_ref[...] = ka
    ka_b_ref[...] = ka.astype(BF16)
    va = proj(2 * W_A, 3 * W_A)
    va_ref[...] = va
    va_b_ref[...] = va.astype(BF16)
    o = 3 * W_A
    qb_ref[...] = proj(o, o + W_B) * Q_SCALE
    kb = proj(o + W_B, o + 2 * W_B)
    kb_ref[...] = kb
    kb_b_ref[...] = kb.astype(BF16)
    km_ref[0] = jnp.mean(kb, axis=0, keepdims=True)
    vb = proj(o + 2 * W_B, o + 3 * W_B)
    vb_ref[...] = vb
    vb_b_ref[...] = vb.astype(BF16)
    z_ref[...] = proj(o + 3 * W_B, o + 3 * W_B + D_MIX)


def _inproj(x2, g_mix, w_in_b):
    R, D = x2.shape
    d_in = w_in_b.shape[1]
    ts = ROW_TILE
    nt = R // ts
    row = lambda w: pl.BlockSpec((ts, w), lambda i: (i, 0))
    f32o = lambda w: jax.ShapeDtypeStruct((R, w), F32)
    b16o = lambda w: jax.ShapeDtypeStruct((R, w), BF16)
    out_shape = (f32o(W_A), f32o(W_A), f32o(W_B), f32o(W_B), f32o(D_MIX), f32o(W_A), f32o(W_B),
                 b16o(W_A), b16o(W_A), b16o(W_B), b16o(W_B),
                 jax.ShapeDtypeStruct((nt, 1, W_B), F32))
    out_specs = (row(W_A), row(W_A), row(W_B), row(W_B), row(D_MIX), row(W_A), row(W_B),
                 row(W_A), row(W_A), row(W_B), row(W_B),
                 pl.BlockSpec((1, 1, W_B), lambda i: (i, 0, 0)))
    work = 2 * (ts * D * 4 + D * d_in * 2 + ts * (6 * 512 + D_MIX) * 4 + ts * 4 * 512 * 2)
    return pl.pallas_call(
        _inproj_kernel,
        grid=(nt,),
        in_specs=[row(D), pl.BlockSpec((1, D), lambda i: (0, 0)), pl.BlockSpec((D, d_in), lambda i: (0, 0))],
        out_specs=out_specs,
        out_shape=out_shape,
        compiler_params=pltpu.CompilerParams(dimension_semantics=("arbitrary",), vmem_limit_bytes=_vmem_limit(work)),
        name="inproj",
    )(x2, g_mix.reshape(1, D), w_in_b)


def _bias_of_dist(dist, tab_ref, h):
    v = jnp.full(dist.shape, tab_ref[h, N_BUCKETS - 1], F32)
    for b, hi in reversed(BUCKET_HI):
        v = jnp.where(dist <= hi, tab_ref[h, b], v)
    return v


def _softmax_tile(s, vt, m_ref, acc_ref, colsel):
    m_prev = m_ref[...]
    m_tile = jnp.max(s, axis=0, keepdims=True)
    if colsel is not None:
        m_tile = jnp.where(colsel, m_tile, NEG_INF)
    m_new = jnp.maximum(m_prev, m_tile)
    m_safe = jnp.where(m_new == NEG_INF, 0.0, m_new)
    alpha = jnp.exp2(m_prev - m_safe)
    sub = m_safe if colsel is None else jnp.where(colsel, m_safe, POS_INF)
    p = jnp.exp2(s - sub).astype(BF16)
    acc_ref[...] = acc_ref[...] * alpha + jnp.dot(vt, p, preferred_element_type=F32)
    m_ref[...] = m_new


def _prompt_attn_kernel(*refs, moba, lam_init):
    if moba:
        (tab_ref, q_ref, k_ref, vt_ref, km_ref, out_ref, m_ref, acc_ref, bias_ref, sel_ref) = refs
    else:
        (tab_ref, q_ref, k_ref, vt_ref, lam_ref, g_ref, out_ref, m_ref, acc_ref, bias_ref) = refs
    i = pl.program_id(1)
    ns = vt_ref.shape[0]
    tq = q_ref.shape[0]
    tk = vt_ref.shape[-1]
    half = LANES // 2

    @pl.when(i == 0)
    def _():
        kk = lax.broadcasted_iota(jnp.int32, (tk, tq), 0)
        qq = lax.broadcasted_iota(jnp.int32, (tk, tq), 1)
        for s in range(ns):
            for c, h in enumerate((2 * s, 2 * s + 1) if moba else (s,)):
                bias_ref[s, 0, :, c * tq:(c + 1) * tq] = _bias_of_dist(qq + tk - kk, tab_ref, h)
                d = qq - kk
                bias_ref[s, 1, :, c * tq:(c + 1) * tq] = jnp.where(d >= 0, _bias_of_dist(d, tab_ref, h), NEG_INF)
            if not moba:
                bias_ref[s, 0, :, tq:] = bias_ref[s, 0, :, :tq]
                bias_ref[s, 1, :, tq:] = bias_ref[s, 1, :, :tq]

    qts, q2 = [], []
    for s in range(ns):
        qt = q_ref[:, s * LANES:(s + 1) * LANES].T
        feat = lax.broadcasted_iota(jnp.int32, qt.shape, 0)
        qts.append(qt)
        q2.append(jnp.concatenate([jnp.where(feat < half, qt, 0.0), jnp.where(feat >= half, qt, 0.0)],
                                  axis=1).astype(BF16))

    m_ref[...] = jnp.full(m_ref.shape, NEG_INF, F32)
    acc_ref[...] = jnp.zeros(acc_ref.shape, F32)

    if moba:
        nb = km_ref.shape[0]
        hi = lax.Precision.HIGHEST
        for s in range(ns):
            km = km_ref[:, s * LANES:(s + 1) * LANES]
            col = lax.broadcasted_iota(jnp.int32, km.shape, 1)
            g = jnp.concatenate(
                [jnp.dot(jnp.where(col < half, km, 0.0), qts[s], precision=hi, preferred_element_type=F32),
                 jnp.dot(jnp.where(col >= half, km, 0.0), qts[s], precision=hi, preferred_element_type=F32)], axis=1)
            blk = lax.broadcasted_iota(jnp.int32, g.shape, 0)
            past = blk < i
            g = jnp.where(past, g, NEG_INF)
            sel = jnp.zeros(g.shape, F32)
            for _ in range(min(MOBA_TOPK, nb - 1)):
                mx = jnp.max(g, axis=0, keepdims=True)
                idx = jnp.min(jnp.where(g == mx, blk, nb), axis=0, keepdims=True)
                hit = blk == idx
                sel = jnp.where(jnp.logical_and(hit, past), 1.0, sel)
                g = jnp.where(hit, NEG_INF, g)
            for r in range(nb):
                sel_ref[s, r] = sel[r:r + 1, :]

    def update(j, bias_idx, masked):
        rows = pl.ds(pl.multiple_of(j * tk, tk), tk)
        scores = [jnp.dot(k_ref[rows, s * LANES:(s + 1) * LANES], q2[s], preferred_element_type=F32)
                  for s in range(ns)]
        for s in range(ns):
            sc = scores[s] if bias_idx is None else scores[s] + bias_ref[s, bias_idx]
            colsel = (sel_ref[s, j] > 0.5) if (moba and masked) else None
            _softmax_tile(sc, vt_ref[s, j], m_ref.at[s], acc_ref.at[s], colsel)

    def far_body(j, carry):
        update(j, None, True)
        return carry

    lax.fori_loop(0, jnp.maximum(i - 1, 0), far_body, 0)

    @pl.when(i >= 1)
    def _():
        update(i - 1, 0, True)

    update(i, 1, False)

    if not moba:
        lam = lam_ref[...]
        lam_l = (jnp.exp(jnp.sum(lam[0:1] * lam[1:2], axis=-1, keepdims=True))
                 - jnp.exp(jnp.sum(lam[2:3] * lam[3:4], axis=-1, keepdims=True)) + lam_init)
    for s in range(ns):
        acc = acc_ref[s]
        inv = 1.0 / acc[LANES:LANES + 1, :]
        if moba:
            o = jnp.concatenate([acc[:half, :tq] * inv[:, :tq], acc[half:LANES, tq:] * inv[:, tq:]], axis=0)
        else:
            o = acc[:LANES, :tq] * inv[:, :tq] - lam_l * (acc[:LANES, tq:] * inv[:, tq:])
            o = o * lax.rsqrt(jnp.mean(o * o, axis=0, keepdims=True) + EPS)
            o = (o * g_ref[...]) * (1.0 - lam_init)
        out_ref[:, s * LANES:(s + 1) * LANES] = o.T


def _values_transposed(v_b, nt, t):
    b = v_b.shape[0]
    vt = v_b.reshape(b, nt, t, N_STREAMS, LANES).transpose(0, 3, 1, 4, 2)
    ones = jnp.zeros((b, N_STREAMS, nt, BF16_SUBLANES, t), BF16).at[:, :, :, 0, :].set(1.0)
    return jnp.concatenate([vt, ones], axis=3)


def _prompt_attn(q_f, k_b, v_b, tab, *, moba, lam_init, lam=None, subln_g=None, kmean=None):
    B, S, W = q_f.shape
    t = ATT_TILE
    nt = S // t
    ns = N_STREAMS
    vt = _values_transposed(v_b, nt, t)
    vrows = LANES + BF16_SUBLANES
    sq = pl.Squeezed()
    in_specs = [
        pl.BlockSpec(memory_space=pltpu.SMEM),
        pl.BlockSpec((sq, t, W), lambda b, i: (b, i, 0)),
        pl.BlockSpec((sq, S, W), lambda b, i: (b, 0, 0), pipeline_mode=pl.Buffered(1)),
        pl.BlockSpec((sq, ns, nt, vrows, t), lambda b, i: (b, 0, 0, 0, 0), pipeline_mode=pl.Buffered(1)),
    ]
    scratch = [pltpu.VMEM((ns, 1, 2 * t), F32), pltpu.VMEM((ns, vrows, 2 * t), F32),
               pltpu.VMEM((ns, 2, t, 2 * t), F32)]
    sel_bytes = 0
    if moba:
        nb = kmean.shape[1]
        args = (tab, q_f, k_b, vt, kmean)
        in_specs += [pl.BlockSpec((sq, nb, W), lambda b, i: (b, 0, 0))]
        scratch += [pltpu.VMEM((ns, nb, 1, 2 * t), F32)]
        sel_bytes = ns * nb * 8 * 2 * t * 4
    else:
        args = (tab, q_f, k_b, vt, lam, subln_g.reshape(LANES, 1))
        in_specs += [pl.BlockSpec(lam.shape, lambda b, i: (0, 0)),
                     pl.BlockSpec((LANES, 1), lambda b, i: (0, 0))]
    work = (S * W * 2 + ns * S * vrows * 2 + 2 * 2 * t * W * 4
            + ns * (2 * t * 2 * t * 4 + vrows * 2 * t * 4) + sel_bytes + 8 * t * 2 * t * 4)
    return pl.pallas_call(
        functools.partial(_prompt_attn_kernel, moba=moba, lam_init=lam_init),
        grid=(B, nt),
        in_specs=in_specs,
        out_specs=pl.BlockSpec((sq, t, W), lambda b, i: (b, i, 0)),
        out_shape=jax.ShapeDtypeStruct((B, S, W), F32),
        scratch_shapes=scratch,
        compiler_params=pltpu.CompilerParams(dimension_semantics=("arbitrary", "arbitrary"),
                                             vmem_limit_bytes=_vmem_limit(work)),
        name="prompt_moba" if moba else "prompt_diff",
    )(*args)


def _nt_dot(a, b):
    return lax.dot_general(a, b, (((1,), (1,)), ((), ())), preferred_element_type=F32)


def _row_softmax_tile(s, v, m_ref, l_ref, acc_ref):
    m_prev = m_ref[...]
    m_new = jnp.maximum(m_prev, jnp.max(s, axis=-1, keepdims=True))
    alpha = jnp.exp2(m_prev - m_new)
    p = jnp.exp2(s - m_new)
    l_ref[...] = alpha * l_ref[...] + jnp.sum(p, axis=-1, keepdims=True)
    acc_ref[...] = acc_ref[...] * alpha + jnp.dot(p.astype(BF16), v, preferred_element_type=F32)
    m_ref[...] = m_new


def _sample_diff_kernel(pt_ref, qbd_ref, *rest, n_pages_step, lam_init):
    g_ = n_pages_step
    k_pages = rest[:g_]
    v_pages = rest[g_:2 * g_]
    (knew_ref, vnew_ref, bnear_ref, bnew_ref, lam_ref, gln_ref, out_ref,
     kbuf, vbuf, m_ref, l_ref, acc_ref) = rest[2 * g_:]
    step = pl.program_id(1)
    last = pl.num_programs(1) - 1
    page = k_pages[0].shape[0] // H_A

    @pl.when(step == 0)
    def _():
        m_ref[...] = jnp.full(m_ref.shape, NEG_INF, F32)
        l_ref[...] = jnp.zeros(l_ref.shape, F32)
        acc_ref[...] = jnp.zeros(acc_ref.shape, F32)

    def regroup(pages, buf, p):
        for h in range(H_A):
            rows = pl.ds(h, page, stride=H_A)
            buf[p * page:(p + 1) * page, h * LANES:(h + 1) * LANES] = pages[p][rows, :].astype(BF16)

    qbd = qbd_ref[...]
    cw = bnear_ref.shape[1]
    ppd = cw // page
    scores = []
    for c in range(g_ // ppd):
        for p in range(c * ppd, (c + 1) * ppd):
            regroup(k_pages, kbuf, p)
        scores.append(_nt_dot(qbd, kbuf[c * cw:(c + 1) * cw, :]))
    for p in range(g_):
        regroup(v_pages, vbuf, p)
    scores[-1] = scores[-1] + bnear_ref[...] * (step == last).astype(F32)
    m_prev = m_ref[...]
    m_new = m_prev
    for sc in scores:
        m_new = jnp.maximum(m_new, jnp.max(sc, axis=-1, keepdims=True))
    alpha = jnp.exp2(m_prev - m_new)
    probs = [jnp.exp2(sc - m_new) for sc in scores]
    l_new = alpha * l_ref[...]
    acc = acc_ref[...] * alpha
    for c, p in enumerate(probs):
        l_new = l_new + jnp.sum(p, axis=-1, keepdims=True)
        acc = acc + jnp.dot(p.astype(BF16), vbuf[c * cw:(c + 1) * cw, :], preferred_element_type=F32)
    l_ref[...] = l_new
    acc_ref[...] = acc
    m_ref[...] = m_new

    @pl.when(step == last)
    def _():
        s_new = _nt_dot(qbd, knew_ref[...]) + bnew_ref[...]
        _row_softmax_tile(s_new, vnew_ref[...], m_ref, l_ref, acc_ref)
        acc = acc_ref[...]
        inv = 1.0 / l_ref[...]
        lam = lam_ref[...]
        lam_l = (jnp.exp(jnp.sum(lam[0:1] * lam[1:2], axis=-1, keepdims=True))
                 - jnp.exp(jnp.sum(lam[2:3] * lam[3:4], axis=-1, keepdims=True)) + lam_init)
        t = out_ref.shape[0]
        rows = 2 * t
        for h in range(H_A):
            blk = acc[h * rows:(h + 1) * rows, h * LANES:(h + 1) * LANES] * inv[h * rows:(h + 1) * rows]
            o = blk[:t] - lam_l * blk[t:]
            o = o * lax.rsqrt(jnp.mean(o * o, axis=-1, keepdims=True) + EPS)
            out_ref[:, h * LANES:(h + 1) * LANES] = (o * gln_ref[...]) * (1.0 - lam_init)


def _sample_moba_kernel(pt_ref, qbd_ref, qbf_ref, *rest, n_pages_step):
    g_ = n_pages_step
    k_pages = rest[:g_]
    v_pages = rest[g_:2 * g_]
    (knew_ref, vnew_ref, bnear_ref, bnew_ref, out_ref,
     kbuf, vbuf, pm_ref, pl_ref, po_ref, km_ref) = rest[2 * g_:]
    step = pl.program_id(1)
    last = pl.num_programs(1) - 1
    page = k_pages[0].shape[1]
    ppb = MOBA_BLOCK // page
    bps = g_ // ppb
    nb = po_ref.shape[0]

    @pl.when(step == 0)
    def _():
        km_ref[...] = jnp.zeros(km_ref.shape, F32)
        pm_ref[...] = jnp.full(pm_ref.shape, NEG_INF, F32)
        pl_ref[...] = jnp.zeros(pl_ref.shape, F32)

    qbd = qbd_ref[...]
    cw = PAGES_PER_DOT * page if g_ % PAGES_PER_DOT == 0 else g_ * page
    ppd = cw // page
    scores = []
    for c in range(g_ // ppd):
        for p in range(c * ppd, (c + 1) * ppd):
            kbuf[:, p * page:(p + 1) * page] = k_pages[p][...].astype(BF16)
        scores.append(jnp.dot(qbd, kbuf[:, c * cw:(c + 1) * cw], preferred_element_type=F32))
    for p in range(g_):
        vbuf[:, p * page:(p + 1) * page] = v_pages[p][...].astype(BF16)

    near = (step == last).astype(F32)
    km_lane = lax.broadcasted_iota(jnp.int32, km_ref.shape, 1)
    stat_lane = lax.broadcasted_iota(jnp.int32, pm_ref.shape, 1)
    km_new, pm_new, pl_new = km_ref[...], pm_ref[...], pl_ref[...]
    probs = []
    for jb in range(bps):
        j = step * bps + jb
        ksum = k_pages[jb * ppb][...]
        for q in range(1, ppb):
            ksum = ksum + k_pages[jb * ppb + q][...]
        kmean = jnp.sum(ksum, axis=-1, keepdims=True) * (1.0 / MOBA_BLOCK)
        km_new = jnp.where(km_lane == j, kmean, km_new)
        off = jb * MOBA_BLOCK - (jb * MOBA_BLOCK // cw) * cw
        s = scores[jb * MOBA_BLOCK // cw][:, off:off + MOBA_BLOCK]
        if jb == bps - 1:
            s = s + bnear_ref[...] * near
        m = jnp.max(s, axis=-1, keepdims=True)
        p = jnp.exp2(s - m)
        pm_new = jnp.where(stat_lane == j, m, pm_new)
        pl_new = jnp.where(stat_lane == j, jnp.sum(p, axis=-1, keepdims=True), pl_new)
        probs.append(p.astype(BF16))
    km_ref[...] = km_new
    pm_ref[...] = pm_new
    pl_ref[...] = pl_new
    for jb in range(bps):
        po_ref[step * bps + jb] = _nt_dot(probs[jb], vbuf[:, jb * MOBA_BLOCK:(jb + 1) * MOBA_BLOCK])

    @pl.when(step == last)
    def _():
        gate = jnp.dot(qbf_ref[...], km_ref[...], precision=lax.Precision.HIGHEST, preferred_element_type=F32)
        blk = lax.broadcasted_iota(jnp.int32, gate.shape, 1)
        gate = jnp.where(blk < nb, gate, NEG_INF)
        sel = jnp.zeros(gate.shape, F32)
        for _ in range(min(MOBA_TOPK, nb)):
            mx = jnp.max(gate, axis=-1, keepdims=True)
            idx = jnp.min(jnp.where(gate == mx, blk, LANES), axis=-1, keepdims=True)
            hit = blk == idx
            sel = jnp.where(hit, 1.0, sel)
            gate = jnp.where(hit, NEG_INF, gate)
        s_own = _nt_dot(qbd, knew_ref[...]) + bnew_ref[...]
        m_own = jnp.max(s_own, axis=-1, keepdims=True)
        p_own = jnp.exp2(s_own - m_own)
        chosen = sel > 0.5
        m = jnp.maximum(m_own, jnp.max(jnp.where(chosen, pm_ref[...], NEG_INF), axis=-1, keepdims=True))
        w = jnp.where(chosen, jnp.exp2(pm_ref[...] - m), 0.0)
        a_own = jnp.exp2(m_own - m)
        l = a_own * jnp.sum(p_own, axis=-1, keepdims=True) + jnp.sum(w * pl_ref[...], axis=-1, keepdims=True)
        o = a_own * jnp.dot(p_own.astype(BF16), vnew_ref[...], preferred_element_type=F32)
        for j in range(nb):
            o = o + w[:, j:j + 1] * po_ref[j]
        o = o * (1.0 / l)
        t = out_ref.shape[0]
        lane_head = lax.broadcasted_iota(jnp.int32, (t, W_B), 1) // DH_B
        y = jnp.zeros((t, W_B), F32)
        for h in range(H_B):
            y = jnp.where(lane_head == h, o[h * t:(h + 1) * t, :], y)
        out_ref[...] = y


def _page_specs(g_, n_pages, block):
    def spec(p):
        return pl.BlockSpec((pl.Squeezed(),) + block, lambda n, s, pt: (pt[n * n_pages + s * g_ + p], 0, 0))
    return [spec(p) for p in range(g_)] + [spec(p) for p in range(g_)]


def _pad_new(a):
    ns, t, w = a.shape
    return jnp.pad(a, ((0, 0), (0, NEW_PAD - t), (0, 0))).astype(BF16)


def _sample_diff(qbd, page_table, cache_k, cache_v, knew, vnew, bnear, bnew, lam, subln_g, lam_init):
    ns, n_pages = page_table.shape
    npool, page, nh, dh = cache_k.shape
    width = nh * dh
    g_ = min(PAGES_PER_STEP, n_pages)
    ck = cache_k.reshape(npool, page * nh, dh)
    cv = cache_v.reshape(npool, page * nh, dh)
    t = knew.shape[1]
    rows = qbd.shape[1]
    c = g_ * page
    sq = pl.Squeezed()
    full = lambda a: pl.BlockSpec(a.shape, lambda n, s, pt: (0,) * a.ndim)
    per_seq = lambda a: pl.BlockSpec((sq,) + a.shape[1:], lambda n, s, pt: (n,) + (0,) * (a.ndim - 1))
    knew_p, vnew_p = _pad_new(knew), _pad_new(vnew)
    gln = subln_g.reshape(1, LANES)
    grid_spec = pltpu.PrefetchScalarGridSpec(
        num_scalar_prefetch=1,
        grid=(ns, n_pages // g_),
        in_specs=[per_seq(qbd)] + _page_specs(g_, n_pages, (page * nh, dh))
                 + [per_seq(knew_p), per_seq(vnew_p), full(bnear), full(bnew), full(lam), full(gln)],
        out_specs=pl.BlockSpec((sq, t, width), lambda n, s, pt: (n, 0, 0)),
        scratch_shapes=[pltpu.VMEM((c, width), BF16), pltpu.VMEM((c, width), BF16),
                        pltpu.VMEM((rows, 1), F32), pltpu.VMEM((rows, 1), F32), pltpu.VMEM((rows, width), F32)],
    )
    work = 2 * 2 * g_ * page * width * 4 + 2 * c * width * 2 + 4 * rows * c * 4
    return pl.pallas_call(
        functools.partial(_sample_diff_kernel, n_pages_step=g_, lam_init=lam_init),
        grid_spec=grid_spec,
        out_shape=jax.ShapeDtypeStruct((ns, t, width), F32),
        compiler_params=pltpu.CompilerParams(dimension_semantics=("arbitrary", "arbitrary"),
                                             vmem_limit_bytes=_vmem_limit(work)),
        name="sample_diff",
    )(page_table.reshape(-1), qbd, *([ck] * g_), *([cv] * g_), knew_p, vnew_p, bnear, bnew, lam, gln)


def _sample_moba(qbd, qbf, page_table, cache_k, cache_v, knew, vnew, bnear, bnew):
    ns, n_pages = page_table.shape
    npool, page, nh, dh = cache_k.shape
    width = nh * dh
    g_ = min(PAGES_PER_STEP, n_pages)
    ck = cache_k.transpose(0, 2, 3, 1).reshape(npool, width, page)
    cv = cache_v.transpose(0, 2, 3, 1).reshape(npool, width, page)
    t = knew.shape[1]
    rows = qbd.shape[1]
    c = g_ * page
    nb = n_pages * page // MOBA_BLOCK
    sq = pl.Squeezed()
    full = lambda a: pl.BlockSpec(a.shape, lambda n, s, pt: (0,) * a.ndim)
    per_seq = lambda a: pl.BlockSpec((sq,) + a.shape[1:], lambda n, s, pt: (n,) + (0,) * (a.ndim - 1))
    knew_p, vnew_p = _pad_new(knew), _pad_new(vnew)
    grid_spec = pltpu.PrefetchScalarGridSpec(
        num_scalar_prefetch=1,
        grid=(ns, n_pages // g_),
        in_specs=[per_seq(qbd), per_seq(qbf)] + _page_specs(g_, n_pages, (width, page))
                 + [per_seq(knew_p), per_seq(vnew_p), full(bnear), full(bnew)],
        out_specs=pl.BlockSpec((sq, t, width), lambda n, s, pt: (n, 0, 0)),
        scratch_shapes=[pltpu.VMEM((width, c), BF16), pltpu.VMEM((width, c), BF16),
                        pltpu.VMEM((rows, LANES), F32), pltpu.VMEM((rows, LANES), F32),
                        pltpu.VMEM((nb, rows, width), F32), pltpu.VMEM((width, LANES), F32)],
    )
    work = (2 * 2 * g_ * page * width * 4 + 2 * c * width * 2 + nb * rows * (width + 2 * LANES) * 4
            + 8 * rows * width * 4)
    return pl.pallas_call(
        functools.partial(_sample_moba_kernel, n_pages_step=g_),
        grid_spec=grid_spec,
        out_shape=jax.ShapeDtypeStruct((ns, t, width), F32),
        compiler_params=pltpu.CompilerParams(dimension_semantics=("arbitrary", "arbitrary"),
                                             vmem_limit_bytes=_vmem_limit(work)),
        name="sample_moba",
    )(page_table.reshape(-1), qbd, qbf, *([ck] * g_), *([cv] * g_), knew_p, vnew_p, bnear, bnew)


def _block_diag_q(q, n_heads):
    ns, t, w = q.shape
    groups = 2 if n_heads == H_A else 1
    n_sl = n_heads * groups
    sl = w // n_sl
    owner = jnp.arange(w) // sl
    rows = jnp.arange(n_sl)
    mask = (owner[None, :] == rows[:, None]).astype(q.dtype)
    out = q[:, None, :, :] * mask[None, :, None, :]
    return out.reshape(ns, n_sl * t, w)


def _sample_bias_tables(tabp, groups, t, near_len):
    n_heads = tabp.shape[0]
    tt = np.arange(t)
    c = np.arange(near_len)
    b_near = _bucket_of_dist(near_len + tt[:, None] - c[None, :])
    tn = np.arange(NEW_PAD)
    d_new = tt[:, None] - tn[None, :]
    b_new = _bucket_of_dist(d_new)
    valid = (d_new >= 0) & (tn[None, :] < t)
    near = tabp[:, b_near]
    new = jnp.where(valid[None], tabp[:, b_new], NEG_INF)
    rep = lambda a: jnp.broadcast_to(a[:, None], (n_heads, groups) + a.shape[1:]).reshape(
        n_heads * groups * t, a.shape[-1])
    return rep(near).astype(F32), rep(new).astype(F32)


def _outproj_kernel(x_ref, ya_ref, yb_ref, z_ref, p_ref, wo_ref, wpg_ref, wple_ref, gple_ref, gfin_ref, out_ref, *,
                    final):
    z = z_ref[...]
    y = jnp.concatenate([ya_ref[...], yb_ref[...]], axis=1)
    yg = (y * (z * jax.nn.sigmoid(z))).astype(BF16)
    h = x_ref[...] + jnp.dot(yg, wo_ref[...], preferred_element_type=F32)
    hn = h * lax.rsqrt(jnp.mean(h * h, axis=-1, keepdims=True) + EPS)
    hn = (hn * gple_ref[...]).astype(BF16)
    gate = jax.nn.sigmoid(jnp.dot(hn, wpg_ref[...], preferred_element_type=F32))
    ple = jnp.dot(p_ref[...].astype(BF16), wple_ref[...], preferred_element_type=F32)
    h = h + gate * ple
    if final:
        h = h * lax.rsqrt(jnp.mean(h * h, axis=-1, keepdims=True) + EPS)
        h = h * gfin_ref[...]
    out_ref[...] = h


def _outproj(x2, ya2, yb2, z2, p2, wo_b, wpg_b, wple_b, g_ple, g_final, final):
    R, D = x2.shape
    ts = ROW_TILE
    row = lambda w: pl.BlockSpec((ts, w), lambda i: (i, 0))
    full = lambda a: pl.BlockSpec(a.shape, lambda i: (0, 0))
    gp, gf = g_ple.reshape(1, D), g_final.reshape(1, D)
    work = 2 * (ts * (2 * D + 2 * D_MIX + p2.shape[1]) * 4 + (D_MIX * D + D * D + p2.shape[1] * D) * 2)
    return pl.pallas_call(
        functools.partial(_outproj_kernel, final=final),
        grid=(R // ts,),
        in_specs=[row(D), row(W_A), row(W_B), row(D_MIX), row(p2.shape[1]), full(wo_b), full(wpg_b), full(wple_b),
                  full(gp), full(gf)],
        out_specs=row(D),
        out_shape=jax.ShapeDtypeStruct((R, D), F32),
        compiler_params=pltpu.CompilerParams(dimension_semantics=("arbitrary",), vmem_limit_bytes=_vmem_limit(work)),
        name="outproj",
    )(x2, ya2, yb2, z2, p2, wo_b, wpg_b, wple_b, gp, gf)


def kernel(x_prompt, x_sample, cache_diff_k, cache_diff_v, cache_moba_k, cache_moba_v, page_table, p_prompt, p_sample, w_in, w_out, g_mix, lam, subln_g, rel_bias, w_ple, w_pg, g_ple, g_final):
    depth = w_in.shape[0]
    B, S, D = x_prompt.shape
    NS, T, _ = x_sample.shape
    n_pages = page_table.shape[1]
    page = cache_diff_k.shape[2]
    past = n_pages * page
    g_pages = min(PAGES_PER_STEP, n_pages)
    assert S % ATT_TILE == 0 and ATT_TILE == MOBA_BLOCK == ROW_TILE and W_A == W_B == N_STREAMS * LANES
    assert (B * S) % ROW_TILE == 0 and (NS * T) % ROW_TILE == 0
    assert past % MOBA_BLOCK == 0 and MOBA_BLOCK % page == 0 and T <= page and T <= NEW_PAD
    assert n_pages % g_pages == 0 and (g_pages * page) % MOBA_BLOCK == 0 and (PAGES_PER_DOT * page) % MOBA_BLOCK == 0
    assert MOBA_TOPK <= past // MOBA_BLOCK <= LANES and ATT_TILE >= MAX_DIST

    tabp = ((rel_bias - rel_bias[N_BUCKETS - 1:N_BUCKETS, :]) * LOG2E).T.astype(F32)
    tab_a, tab_b = tabp[:H_A], tabp[H_A:]
    near_d, new_d = _sample_bias_tables(tab_a, 2, T, page)
    dot_pages = PAGES_PER_DOT if g_pages % PAGES_PER_DOT == 0 else g_pages
    near_d = jnp.pad(near_d, ((0, 0), (dot_pages * page - page, 0)))
    near_m, new_m = _sample_bias_tables(tab_b, 1, T, MOBA_BLOCK)

    hp, hs = x_prompt, x_sample
    outs = [[] for _ in range(8)]
    for layer in range(depth):
        lam_init = 0.8 - 0.6 * math.exp(-0.3 * layer)
        final = layer == depth - 1
        w_in_b = w_in[layer].astype(BF16)
        wo_b, wpg_b, wple_b = w_out[layer].astype(BF16), w_pg[layer].astype(BF16), w_ple[layer].astype(BF16)

        (ka, va, kb, vb, z, qa_f, qb_f, ka_b, va_b, kb_b, vb_b, km) = _inproj(
            hp.reshape(B * S, D), g_mix[layer], w_in_b)
        r3 = lambda a: a.reshape(B, S, a.shape[-1])
        ya = _prompt_attn(r3(qa_f), r3(ka_b), r3(va_b), tab_a, moba=False, lam_init=lam_init,
                          lam=lam[layer], subln_g=subln_g[layer])
        yb = _prompt_attn(r3(qb_f), r3(kb_b), r3(vb_b), tab_b, moba=True, lam_init=lam_init,
                          kmean=km.reshape(B, S // MOBA_BLOCK, W_B))
        hp = _outproj(hp.reshape(B * S, D), ya.reshape(B * S, W_A), yb.reshape(B * S, W_B), z,
                      p_prompt[layer].reshape(B * S, -1), wo_b, wpg_b, wple_b,
                      g_ple[layer], g_final, final).reshape(B, S, D)
        outs[0].append(ka.reshape(B, S, H_A, 2 * DH_A)); outs[1].append(va.reshape(B, S, H_A, 2 * DH_A))
        outs[2].append(kb.reshape(B, S, H_B, DH_B)); outs[3].append(vb.reshape(B, S, H_B, DH_B))

        (ka, va, kb, vb, z, qa_f, qb_f, _, _, _, _, _) = _inproj(hs.reshape(NS * T, D), g_mix[layer], w_in_b)
        s3 = lambda a: a.reshape(NS, T, a.shape[-1])
        ya = _sample_diff(_block_diag_q(s3(qa_f).astype(BF16), H_A), page_table, cache_diff_k[layer],
                          cache_diff_v[layer], s3(ka), s3(va), near_d, new_d, lam[layer], subln_g[layer], lam_init)
        yb = _sample_moba(_block_diag_q(s3(qb_f).astype(BF16), H_B), _block_diag_q(s3(qb_f), H_B), page_table,
                          cache_moba_k[layer], cache_moba_v[layer], s3(kb), s3(vb), near_m, new_m)
        hs = _outproj(hs.reshape(NS * T, D), ya.reshape(NS * T, W_A), yb.reshape(NS * T, W_B), z,
                      p_sample[layer].reshape(NS * T, -1), wo_b, wpg_b, wple_b,
                      g_ple[layer], g_final, final).reshape(NS, T, D)
        outs[4].append(ka.reshape(NS, T, H_A, 2 * DH_A)); outs[5].append(va.reshape(NS, T, H_A, 2 * DH_A))
        outs[6].append(kb.reshape(NS, T, H_B, DH_B)); outs[7].append(vb.reshape(NS, T, H_B, DH_B))

    return (hp, hs) + tuple(jnp.stack(o) for o in outs)
```

```python
import functools
import math

import jax
import jax.numpy as jnp
import numpy as np
from jax import lax
from jax.experimental import pallas as pl
from jax.experimental.pallas import tpu as pltpu

F32 = jnp.float32
BF16 = jnp.bfloat16
NEG_INF = float("-inf")
POS_INF = float("inf")

H_A = 4
DH_A = 64
H_B = 8
DH_B = 64
W_A = H_A * 2 * DH_A
W_B = H_B * DH_B
D_MIX = W_A + W_B
MOBA_BLOCK = 256
MOBA_TOPK = 3
N_BUCKETS = 32
MAX_DIST = 128
EPS = 1e-6
LOG2E = math.log2(math.e)
Q_SCALE = DH_A ** -0.5 * LOG2E

LANES = 128
BF16_SUBLANES = 16
V7X_VMEM_BYTES = 64 * 1024 * 1024
COMPILER_SCRATCH_BYTES = 16 * 1024 * 1024

ROW_TILE = 256
ATT_TILE = 256
N_STREAMS = 4
PAGES_PER_STEP = 32
PAGES_PER_DOT = 8
NEW_PAD = BF16_SUBLANES


def _vmem_limit(nbytes):
    return int(min(nbytes + COMPILER_SCRATCH_BYTES, V7X_VMEM_BYTES - 8 * 1024 * 1024))


def _bucket_of_dist(n):
    n = np.maximum(n, 0)
    max_exact = N_BUCKETS // 2
    nf = np.maximum(n, 1).astype(np.float32)
    large = max_exact + (np.log(nf / np.float32(max_exact)) / np.float32(math.log(MAX_DIST / max_exact))
                         * np.float32(N_BUCKETS - max_exact)).astype(np.int32)
    large = np.minimum(large, N_BUCKETS - 1)
    return np.where(n < max_exact, n, large).astype(np.int32)


def _bucket_upper_bounds():
    b = _bucket_of_dist(np.arange(4 * MAX_DIST))
    return [(k, int(np.nonzero(b <= k)[0].max())) for k in range(N_BUCKETS - 1)]


BUCKET_HI = _bucket_upper_bounds()


def _inproj_kernel(x_ref, g_ref, w_ref, ka_ref, va_ref, kb_ref, vb_ref, z_ref,
                   qa_ref, qb_ref, ka_b_ref, va_b_ref, kb_b_ref, vb_b_ref, km_ref):
    x = x_ref[...]
    xn = x * lax.rsqrt(jnp.mean(x * x, axis=-1, keepdims=True) + EPS)
    xn = (xn * g_ref[...]).astype(BF16)

    def proj(c0, c1):
        return jnp.dot(xn, w_ref[:, c0:c1], preferred_element_type=F32)

    qa_ref[...] = proj(0, W_A) * Q_SCALE
    ka = proj(W_A, 2 * W_A)
    ka_ref[...] = ka
    ka_b_ref[...] = ka.astype(BF16)
    va = proj(2 * W_A, 3 * W_A)
    va_ref[...] = va
    va_b_ref[...] = va.astype(BF16)
    o = 3 * W_A
    qb_ref[...] = proj(o, o + W_B) * Q_SCALE
    kb = proj(o + W_B, o + 2 * W_B)
    kb_ref[...] = kb
    kb_b_ref[...] = kb.astype(BF16)
    km_ref[0] = jnp.mean(kb, axis=0, keepdims=True)
    vb = proj(o + 2 * W_B, o + 3 * W_B)
    vb_ref[...] = vb
    vb_b_ref[...] = vb.astype(BF16)
    z_ref[...] = proj(o + 3 * W_B, o + 3 * W_B + D_MIX)


def _inproj(x2, g_mix, w_in_b):
    R, D = x2.shape
    d_in = w_in_b.shape[1]
    ts = ROW_TILE
    nt = R // ts
    row = lambda w: pl.BlockSpec((ts, w), lambda i: (i, 0))
    f32o = lambda w: jax.ShapeDtypeStruct((R, w), F32)
    b16o = lambda w: jax.ShapeDtypeStruct((R, w), BF16)
    out_shape = (f32o(W_A), f32o(W_A), f32o(W_B), f32o(W_B), f32o(D_MIX), f32o(W_A), f32o(W_B),
                 b16o(W_A), b16o(W_A), b16o(W_B), b16o(W_B),
                 jax.ShapeDtypeStruct((nt, 1, W_B), F32))
    out_specs = (row(W_A), row(W_A), row(W_B), row(W_B), row(D_MIX), row(W_A), row(W_B),
                 row(W_A), row(W_A), row(W_B), row(W_B),
                 pl.BlockSpec((1, 1, W_B), lambda i: (i, 0, 0)))
    work = 2 * (ts * D * 4 + D * d_in * 2 + ts * (6 * 512 + D_MIX) * 4 + ts * 4 * 512 * 2)
    return pl.pallas_call(
        _inproj_kernel,
        grid=(nt,),
        in_specs=[row(D), pl.BlockSpec((1, D), lambda i: (0, 0)), pl.BlockSpec((D, d_in), lambda i: (0, 0))],
        out_specs=out_specs,
        out_shape=out_shape,
        compiler_params=pltpu.CompilerParams(dimension_semantics=("arbitrary",), vmem_limit_bytes=_vmem_limit(work)),
        name="inproj",
    )(x2, g_mix.reshape(1, D), w_in_b)


def _bias_of_dist(dist, tab_ref, h):
    v = jnp.full(dist.shape, tab_ref[h, N_BUCKETS - 1], F32)
    for b, hi in reversed(BUCKET_HI):
        v = jnp.where(dist <= hi, tab_ref[h, b], v)
    return v


def _softmax_tile(s, vt, m_ref, acc_ref, colsel):
    m_prev = m_ref[...]
    m_tile = jnp.max(s, axis=0, keepdims=True)
    if colsel is not None:
        m_tile = jnp.where(colsel, m_tile, NEG_INF)
    m_new = jnp.maximum(m_prev, m_tile)
    m_safe = jnp.where(m_new == NEG_INF, 0.0, m_new)
    alpha = jnp.exp2(m_prev - m_safe)
    sub = m_safe if colsel is None else jnp.where(colsel, m_safe, POS_INF)
    p = jnp.exp2(s - sub).astype(BF16)
    acc_ref[...] = acc_ref[...] * alpha + jnp.dot(vt, p, preferred_element_type=F32)
    m_ref[...] = m_new


def _prompt_attn_kernel(*refs, moba, lam_init):
    if moba:
        (tab_ref, q_ref, k_ref, vt_ref, km_ref, out_ref, m_ref, acc_ref, bias_ref, sc_ref, sel_ref) = refs
    else:
        (tab_ref, q_ref, k_ref, vt_ref, lam_ref, g_ref, out_ref, m_ref, acc_ref, bias_ref, sc_ref) = refs
    i = pl.program_id(1)
    ns = vt_ref.shape[0]
    tq = q_ref.shape[0]
    tk = vt_ref.shape[-1]
    half = LANES // 2

    @pl.when(i == 0)
    def _():
        kk = lax.broadcasted_iota(jnp.int32, (tk, tq), 0)
        qq = lax.broadcasted_iota(jnp.int32, (tk, tq), 1)
        for s in range(ns):
            for c, h in enumerate((2 * s, 2 * s + 1) if moba else (s,)):
                bias_ref[s, 0, :, c * tq:(c + 1) * tq] = _bias_of_dist(qq + tk - kk, tab_ref, h)
                d = qq - kk
                bias_ref[s, 1, :, c * tq:(c + 1) * tq] = jnp.where(d >= 0, _bias_of_dist(d, tab_ref, h), NEG_INF)
            if not moba:
                bias_ref[s, 0, :, tq:] = bias_ref[s, 0, :, :tq]
                bias_ref[s, 1, :, tq:] = bias_ref[s, 1, :, :tq]

    qts, q2 = [], []
    for s in range(ns):
        qt = q_ref[:, s * LANES:(s + 1) * LANES].T
        feat = lax.broadcasted_iota(jnp.int32, qt.shape, 0)
        qts.append(qt)
        q2.append(jnp.concatenate([jnp.where(feat < half, qt, 0.0), jnp.where(feat >= half, qt, 0.0)],
                                  axis=1).astype(BF16))

    m_ref[...] = jnp.full(m_ref.shape, NEG_INF, F32)
    acc_ref[...] = jnp.zeros(acc_ref.shape, F32)

    if moba:
        nb = km_ref.shape[0]
        hi = lax.Precision.HIGHEST
        for s in range(ns):
            km = km_ref[:, s * LANES:(s + 1) * LANES]
            col = lax.broadcasted_iota(jnp.int32, km.shape, 1)
            g = jnp.concatenate(
                [jnp.dot(jnp.where(col < half, km, 0.0), qts[s], precision=hi, preferred_element_type=F32),
                 jnp.dot(jnp.where(col >= half, km, 0.0), qts[s], precision=hi, preferred_element_type=F32)], axis=1)
            blk = lax.broadcasted_iota(jnp.int32, g.shape, 0)
            past = blk < i
            g = jnp.where(past, g, NEG_INF)
            sel = jnp.zeros(g.shape, F32)
            for _ in range(min(MOBA_TOPK, nb - 1)):
                mx = jnp.max(g, axis=0, keepdims=True)
                idx = jnp.min(jnp.where(g == mx, blk, nb), axis=0, keepdims=True)
                hit = blk == idx
                sel = jnp.where(jnp.logical_and(hit, past), 1.0, sel)
                g = jnp.where(hit, NEG_INF, g)
            for r in range(nb):
                sel_ref[s, r] = sel[r:r + 1, :]

    def scores_into(j, slot):
        rows = pl.ds(pl.multiple_of(j * tk, tk), tk)
        for s in range(ns):
            sc_ref[slot, s] = jnp.dot(k_ref[rows, s * LANES:(s + 1) * LANES], q2[s], preferred_element_type=F32)

    def consume(j, slot, bias_idx, masked, bias_flag=None):
        for s in range(ns):
            sc = sc_ref[slot, s]
            if bias_idx is not None:
                sc = sc + (bias_ref[s, bias_idx] if bias_flag is None else bias_ref[s, bias_idx] * bias_flag)
            colsel = (sel_ref[s, j] > 0.5) if (moba and masked) else None
            _softmax_tile(sc, vt_ref[s, j], m_ref.at[s], acc_ref.at[s], colsel)

    scores_into(0, 0)

    def pair_body(jj, carry):
        t0 = 2 * jj
        scores_into(t0 + 1, 1)
        consume(t0, 0, None, True)
        scores_into(t0 + 2, 0)
        consume(t0 + 1, 1, 0, True, bias_flag=(t0 + 1 == i - 1).astype(F32))
        return carry

    lax.fori_loop(0, i // 2, pair_body, 0)
    odd = (i % 2) == 1

    @pl.when(odd)
    def _():
        scores_into(i, 1)
        consume(i - 1, 0, 0, True)
        consume(i, 1, 1, False)

    @pl.when(jnp.logical_not(odd))
    def _():
        consume(i, 0, 1, False)

    if not moba:
        lam = lam_ref[...]
        lam_l = (jnp.exp(jnp.sum(lam[0:1] * lam[1:2], axis=-1, keepdims=True))
                 - jnp.exp(jnp.sum(lam[2:3] * lam[3:4], axis=-1, keepdims=True)) + lam_init)
    for s in range(ns):
        acc = acc_ref[s]
        inv = 1.0 / acc[LANES:LANES + 1, :]
        if moba:
            o = jnp.concatenate([acc[:half, :tq] * inv[:, :tq], acc[half:LANES, tq:] * inv[:, tq:]], axis=0)
        else:
            o = acc[:LANES, :tq] * inv[:, :tq] - lam_l * (acc[:LANES, tq:] * inv[:, tq:])
            o = o * lax.rsqrt(jnp.mean(o * o, axis=0, keepdims=True) + EPS)
            o = (o * g_ref[...]) * (1.0 - lam_init)
        out_ref[:, s * LANES:(s + 1) * LANES] = o.T


def _values_transposed(v_b, nt, t):
    b = v_b.shape[0]
    vt = v_b.reshape(b, nt, t, N_STREAMS, LANES).transpose(0, 3, 1, 4, 2)
    ones = jnp.zeros((b, N_STREAMS, nt, BF16_SUBLANES, t), BF16).at[:, :, :, 0, :].set(1.0)
    return jnp.concatenate([vt, ones], axis=3)


def _prompt_attn(q_f, k_b, v_b, tab, *, moba, lam_init, lam=None, subln_g=None, kmean=None):
    B, S, W = q_f.shape
    t = ATT_TILE
    nt = S // t
    ns = N_STREAMS
    vt = _values_transposed(v_b, nt, t)
    vrows = LANES + BF16_SUBLANES
    sq = pl.Squeezed()
    in_specs = [
        pl.BlockSpec(memory_space=pltpu.SMEM),
        pl.BlockSpec((sq, t, W), lambda b, i: (b, i, 0)),
        pl.BlockSpec((sq, S, W), lambda b, i: (b, 0, 0), pipeline_mode=pl.Buffered(1)),
        pl.BlockSpec((sq, ns, nt, vrows, t), lambda b, i: (b, 0, 0, 0, 0), pipeline_mode=pl.Buffered(1)),
    ]
    scratch = [pltpu.VMEM((ns, 1, 2 * t), F32), pltpu.VMEM((ns, vrows, 2 * t), F32),
               pltpu.VMEM((ns, 2, t, 2 * t), F32), pltpu.VMEM((2, ns, t, 2 * t), F32)]
    sel_bytes = 0
    if moba:
        nb = kmean.shape[1]
        args = (tab, q_f, k_b, vt, kmean)
        in_specs += [pl.BlockSpec((sq, nb, W), lambda b, i: (b, 0, 0))]
        scratch += [pltpu.VMEM((ns, nb, 1, 2 * t), F32)]
        sel_bytes = ns * nb * 8 * 2 * t * 4
    else:
        args = (tab, q_f, k_b, vt, lam, subln_g.reshape(LANES, 1))
        in_specs += [pl.BlockSpec(lam.shape, lambda b, i: (0, 0)),
                     pl.BlockSpec((LANES, 1), lambda b, i: (0, 0))]
    work = (S * W * 2 + ns * S * vrows * 2 + 2 * 2 * t * W * 4
            + ns * (4 * t * 2 * t * 4 + vrows * 2 * t * 4) + sel_bytes + 8 * t * 2 * t * 4)
    return pl.pallas_call(
        functools.partial(_prompt_attn_kernel, moba=moba, lam_init=lam_init),
        grid=(B, nt),
        in_specs=in_specs,
        out_specs=pl.BlockSpec((sq, t, W), lambda b, i: (b, i, 0)),
        out_shape=jax.ShapeDtypeStruct((B, S, W), F32),
        scratch_shapes=scratch,
        compiler_params=pltpu.CompilerParams(dimension_semantics=("arbitrary", "arbitrary"),
                                             vmem_limit_bytes=_vmem_limit(work)),
        name="prompt_moba" if moba else "prompt_diff",
    )(*args)


def _nt_dot(a, b):
    return lax.dot_general(a, b, (((1,), (1,)), ((), ())), preferred_element_type=F32)


def _row_softmax_tile(s, v, m_ref, l_ref, acc_ref):
    m_prev = m_ref[...]
    m_new = jnp.maximum(m_prev, jnp.max(s, axis=-1, keepdims=True))
    alpha = jnp.exp2(m_prev - m_new)
    p = jnp.exp2(s - m_new)
    l_ref[...] = alpha * l_ref[...] + jnp.sum(p, axis=-1, keepdims=True)
    acc_ref[...] = acc_ref[...] * alpha + jnp.dot(p.astype(BF16), v, preferred_element_type=F32)
    m_ref[...] = m_new


def _sample_diff_kernel(pt_ref, qbd_ref, *rest, n_pages_step, lam_init):
    g_ = n_pages_step
    k_pages = rest[:g_]
    v_pages = rest[g_:2 * g_]
    (knew_ref, vnew_ref, bnear_ref, bnew_ref, lam_ref, gln_ref, out_ref,
     kbuf, vbuf, m_ref, l_ref, acc_ref) = rest[2 * g_:]
    step = pl.program_id(1)
    last = pl.num_programs(1) - 1
    page = k_pages[0].shape[0] // H_A

    @pl.when(step == 0)
    def _():
        m_ref[...] = jnp.full(m_ref.shape, NEG_INF, F32)
        l_ref[...] = jnp.zeros(l_ref.shape, F32)
        acc_ref[...] = jnp.zeros(acc_ref.shape, F32)

    def regroup(pages, buf, p):
        for h in range(H_A):
            rows = pl.ds(h, page, stride=H_A)
            buf[p * page:(p + 1) * page, h * LANES:(h + 1) * LANES] = pages[p][rows, :].astype(BF16)

    qbd = qbd_ref[...]
    cw = bnear_ref.shape[1]
    ppd = cw // page
    scores = []
    for c in range(g_ // ppd):
        for p in range(c * ppd, (c + 1) * ppd):
            regroup(k_pages, kbuf, p)
        scores.append(_nt_dot(qbd, kbuf[c * cw:(c + 1) * cw, :]))
    for p in range(g_):
        regroup(v_pages, vbuf, p)
    scores[-1] = scores[-1] + bnear_ref[...] * (step == last).astype(F32)
    m_prev = m_ref[...]
    m_new = m_prev
    for sc in scores:
        m_new = jnp.maximum(m_new, jnp.max(sc, axis=-1, keepdims=True))
    alpha = jnp.exp2(m_prev - m_new)
    probs = [jnp.exp2(sc - m_new) for sc in scores]
    l_new = alpha * l_ref[...]
    acc = acc_ref[...] * alpha
    for c, p in enumerate(probs):
        l_new = l_new + jnp.sum(p, axis=-1, keepdims=True)
        acc = acc + jnp.dot(p.astype(BF16), vbuf[c * cw:(c + 1) * cw, :], preferred_element_type=F32)
    l_ref[...] = l_new
    acc_ref[...] = acc
    m_ref[...] = m_new

    @pl.when(step == last)
    def _():
        s_new = _nt_dot(qbd, knew_ref[...]) + bnew_ref[...]
        _row_softmax_tile(s_new, vnew_ref[...], m_ref, l_ref, acc_ref)
        acc = acc_ref[...]
        inv = 1.0 / l_ref[...]
        lam = lam_ref[...]
        lam_l = (jnp.exp(jnp.sum(lam[0:1] * lam[1:2], axis=-1, keepdims=True))
                 - jnp.exp(jnp.sum(lam[2:3] * lam[3:4], axis=-1, keepdims=True)) + lam_init)
        t = out_ref.shape[0]
        rows = 2 * t
        for h in range(H_A):
            blk = acc[h * rows:(h + 1) * rows, h * LANES:(h + 1) * LANES] * inv[h * rows:(h + 1) * rows]
            o = blk[:t] - lam_l * blk[t:]
            o = o * lax.rsqrt(jnp.mean(o * o, axis=-1, keepdims=True) + EPS)
            out_ref[:, h * LANES:(h + 1) * LANES] = (o * gln_ref[...]) * (1.0 - lam_init)


def _sample_moba_kernel(pt_ref, qbd_ref, qbf_ref, *rest, n_pages_step):
    g_ = n_pages_step
    k_pages = rest[:g_]
    v_pages = rest[g_:2 * g_]
    (knew_ref, vnew_ref, bnear_ref, bnew_ref, out_ref,
     kbuf, vbuf, pm_ref, pl_ref, po_ref, km_ref) = rest[2 * g_:]
    step = pl.program_id(1)
    last = pl.num_programs(1) - 1
    page = k_pages[0].shape[1]
    ppb = MOBA_BLOCK // page
    bps = g_ // ppb
    nb = po_ref.shape[0]

    @pl.when(step == 0)
    def _():
        km_ref[...] = jnp.zeros(km_ref.shape, F32)
        pm_ref[...] = jnp.full(pm_ref.shape, NEG_INF, F32)
        pl_ref[...] = jnp.zeros(pl_ref.shape, F32)

    qbd = qbd_ref[...]
    cw = PAGES_PER_DOT * page if g_ % PAGES_PER_DOT == 0 else g_ * page
    ppd = cw // page
    scores = []
    for c in range(g_ // ppd):
        for p in range(c * ppd, (c + 1) * ppd):
            kbuf[:, p * page:(p + 1) * page] = k_pages[p][...].astype(BF16)
        scores.append(jnp.dot(qbd, kbuf[:, c * cw:(c + 1) * cw], preferred_element_type=F32))
    for p in range(g_):
        vbuf[:, p * page:(p + 1) * page] = v_pages[p][...].astype(BF16)

    near = (step == last).astype(F32)
    km_lane = lax.broadcasted_iota(jnp.int32, km_ref.shape, 1)
    stat_lane = lax.broadcasted_iota(jnp.int32, pm_ref.shape, 1)
    km_new, pm_new, pl_new = km_ref[...], pm_ref[...], pl_ref[...]
    probs = []
    for jb in range(bps):
        j = step * bps + jb
        ksum = k_pages[jb * ppb][...]
        for q in range(1, ppb):
            ksum = ksum + k_pages[jb * ppb + q][...]
        kmean = jnp.sum(ksum, axis=-1, keepdims=True) * (1.0 / MOBA_BLOCK)
        km_new = jnp.where(km_lane == j, kmean, km_new)
        off = jb * MOBA_BLOCK - (jb * MOBA_BLOCK // cw) * cw
        s = scores[jb * MOBA_BLOCK // cw][:, off:off + MOBA_BLOCK]
        if jb == bps - 1:
            s = s + bnear_ref[...] * near
        m = jnp.max(s, axis=-1, keepdims=True)
        p = jnp.exp2(s - m)
        pm_new = jnp.where(stat_lane == j, m, pm_new)
        pl_new = jnp.where(stat_lane == j, jnp.sum(p, axis=-1, keepdims=True), pl_new)
        probs.append(p.astype(BF16))
    km_ref[...] = km_new
    pm_ref[...] = pm_new
    pl_ref[...] = pl_new
    for jb in range(bps):
        po_ref[step * bps + jb] = _nt_dot(probs[jb], vbuf[:, jb * MOBA_BLOCK:(jb + 1) * MOBA_BLOCK])

    @pl.when(step == last)
    def _():
        gate = jnp.dot(qbf_ref[...], km_ref[...], precision=lax.Precision.HIGHEST, preferred_element_type=F32)
        blk = lax.broadcasted_iota(jnp.int32, gate.shape, 1)
        gate = jnp.where(blk < nb, gate, NEG_INF)
        sel = jnp.zeros(gate.shape, F32)
        for _ in range(min(MOBA_TOPK, nb)):
            mx = jnp.max(gate, axis=-1, keepdims=True)
            idx = jnp.min(jnp.where(gate == mx, blk, LANES), axis=-1, keepdims=True)
            hit = blk == idx
            sel = jnp.where(hit, 1.0, sel)
            gate = jnp.where(hit, NEG_INF, gate)
        s_own = _nt_dot(qbd, knew_ref[...]) + bnew_ref[...]
        m_own = jnp.max(s_own, axis=-1, keepdims=True)
        p_own = jnp.exp2(s_own - m_own)
        chosen = sel > 0.5
        m = jnp.maximum(m_own, jnp.max(jnp.where(chosen, pm_ref[...], NEG_INF), axis=-1, keepdims=True))
        w = jnp.where(chosen, jnp.exp2(pm_ref[...] - m), 0.0)
        a_own = jnp.exp2(m_own - m)
        l = a_own * jnp.sum(p_own, axis=-1, keepdims=True) + jnp.sum(w * pl_ref[...], axis=-1, keepdims=True)
        o = a_own * jnp.dot(p_own.astype(BF16), vnew_ref[...], preferred_element_type=F32)
        for j in range(nb):
            o = o + w[:, j:j + 1] * po_ref[j]
        o = o * (1.0 / l)
        t = out_ref.shape[0]
        lane_head = lax.broadcasted_iota(jnp.int32, (t, W_B), 1) // DH_B
        y = jnp.zeros((t, W_B), F32)
        for h in range(H_B):
            y = jnp.where(lane_head == h, o[h * t:(h + 1) * t, :], y)
        out_ref[...] = y


def _page_specs(g_, n_pages, block):
    def spec(p):
        return pl.BlockSpec((pl.Squeezed(),) + block, lambda n, s, pt: (pt[n * n_pages + s * g_ + p], 0, 0))
    return [spec(p) for p in range(g_)] + [spec(p) for p in range(g_)]


def _pad_new(a):
    ns, t, w = a.shape
    return jnp.pad(a, ((0, 0), (0, NEW_PAD - t), (0, 0))).astype(BF16)


def _sample_diff(qbd, page_table, cache_k, cache_v, knew, vnew, bnear, bnew, lam, subln_g, lam_init):
    ns, n_pages = page_table.shape
    npool, page, nh, dh = cache_k.shape
    width = nh * dh
    g_ = min(PAGES_PER_STEP, n_pages)
    ck = cache_k.reshape(npool, page * nh, dh)
    cv = cache_v.reshape(npool, page * nh, dh)
    t = knew.shape[1]
    rows = qbd.shape[1]
    c = g_ * page
    sq = pl.Squeezed()
    full = lambda a: pl.BlockSpec(a.shape, lambda n, s, pt: (0,) * a.ndim)
    per_seq = lambda a: pl.BlockSpec((sq,) + a.shape[1:], lambda n, s, pt: (n,) + (0,) * (a.ndim - 1))
    knew_p, vnew_p = _pad_new(knew), _pad_new(vnew)
    gln = subln_g.reshape(1, LANES)
    grid_spec = pltpu.PrefetchScalarGridSpec(
        num_scalar_prefetch=1,
        grid=(ns, n_pages // g_),
        in_specs=[per_seq(qbd)] + _page_specs(g_, n_pages, (page * nh, dh))
                 + [per_seq(knew_p), per_seq(vnew_p), full(bnear), full(bnew), full(lam), full(gln)],
        out_specs=pl.BlockSpec((sq, t, width), lambda n, s, pt: (n, 0, 0)),
        scratch_shapes=[pltpu.VMEM((c, width), BF16), pltpu.VMEM((c, width), BF16),
                        pltpu.VMEM((rows, 1), F32), pltpu.VMEM((rows, 1), F32), pltpu.VMEM((rows, width), F32)],
    )
    work = 2 * 2 * g_ * page * width * 4 + 2 * c * width * 2 + 4 * rows * c * 4
    return pl.pallas_call(
        functools.partial(_sample_diff_kernel, n_pages_step=g_, lam_init=lam_init),
        grid_spec=grid_spec,
        out_shape=jax.ShapeDtypeStruct((ns, t, width), F32),
        compiler_params=pltpu.CompilerParams(dimension_semantics=("arbitrary", "arbitrary"),
                                             vmem_limit_bytes=_vmem_limit(work)),
        name="sample_diff",
    )(page_table.reshape(-1), qbd, *([ck] * g_), *([cv] * g_), knew_p, vnew_p, bnear, bnew, lam, gln)


def _sample_moba(qbd, qbf, page_table, cache_k, cache_v, knew, vnew, bnear, bnew):
    ns, n_pages = page_table.shape
    npool, page, nh, dh = cache_k.shape
    width = nh * dh
    g_ = min(PAGES_PER_STEP, n_pages)
    ck = cache_k.transpose(0, 2, 3, 1).reshape(npool, width, page)
    cv = cache_v.transpose(0, 2, 3, 1).reshape(npool, width, page)
    t = knew.shape[1]
    rows = qbd.shape[1]
    c = g_ * page
    nb = n_pages * page // MOBA_BLOCK
    sq = pl.Squeezed()
    full = lambda a: pl.BlockSpec(a.shape, lambda n, s, pt: (0,) * a.ndim)
    per_seq = lambda a: pl.BlockSpec((sq,) + a.shape[1:], lambda n, s, pt: (n,) + (0,) * (a.ndim - 1))
    knew_p, vnew_p = _pad_new(knew), _pad_new(vnew)
    grid_spec = pltpu.PrefetchScalarGridSpec(
        num_scalar_prefetch=1,
        grid=(ns, n_pages // g_),
        in_specs=[per_seq(qbd), per_seq(qbf)] + _page_specs(g_, n_pages, (width, page))
                 + [per_seq(knew_p), per_seq(vnew_p), full(bnear), full(bnew)],
        out_specs=pl.BlockSpec((sq, t, width), lambda n, s, pt: (n, 0, 0)),
        scratch_shapes=[pltpu.VMEM((width, c), BF16), pltpu.VMEM((width, c), BF16),
                        pltpu.VMEM((rows, LANES), F32), pltpu.VMEM((rows, LANES), F32),
                        pltpu.VMEM((nb, rows, width), F32), pltpu.VMEM((width, LANES), F32)],
    )
    work = (2 * 2 * g_ * page * width * 4 + 2 * c * width * 2 + nb * rows * (width + 2 * LANES) * 4
            + 8 * rows * width * 4)
    return pl.pallas_call(
        functools.partial(_sample_moba_kernel, n_pages_step=g_),
        grid_spec=grid_spec,
        out_shape=jax.ShapeDtypeStruct((ns, t, width), F32),
        compiler_params=pltpu.CompilerParams(dimension_semantics=("arbitrary", "arbitrary"),
                                             vmem_limit_bytes=_vmem_limit(work)),
        name="sample_moba",
    )(page_table.reshape(-1), qbd, qbf, *([ck] * g_), *([cv] * g_), knew_p, vnew_p, bnear, bnew)


def _block_diag_q(q, n_heads):
    ns, t, w = q.shape
    groups = 2 if n_heads == H_A else 1
    n_sl = n_heads * groups
    sl = w // n_sl
    owner = jnp.arange(w) // sl
    rows = jnp.arange(n_sl)
    mask = (owner[None, :] == rows[:, None]).astype(q.dtype)
    out = q[:, None, :, :] * mask[None, :, None, :]
    return out.reshape(ns, n_sl * t, w)


def _sample_bias_tables(tabp, groups, t, near_len):
    n_heads = tabp.shape[0]
    tt = np.arange(t)
    c = np.arange(near_len)
    b_near = _bucket_of_dist(near_len + tt[:, None] - c[None, :])
    tn = np.arange(NEW_PAD)
    d_new = tt[:, None] - tn[None, :]
    b_new = _bucket_of_dist(d_new)
    valid = (d_new >= 0) & (tn[None, :] < t)
    near = tabp[:, b_near]
    new = jnp.where(valid[None], tabp[:, b_new], NEG_INF)
    rep = lambda a: jnp.broadcast_to(a[:, None], (n_heads, groups) + a.shape[1:]).reshape(
        n_heads * groups * t, a.shape[-1])
    return rep(near).astype(F32), rep(new).astype(F32)


def _outproj_kernel(x_ref, ya_ref, yb_ref, z_ref, p_ref, wo_ref, wpg_ref, wple_ref, gple_ref, gfin_ref, out_ref, *,
                    final):
    z = z_ref[...]
    y = jnp.concatenate([ya_ref[...], yb_ref[...]], axis=1)
    yg = (y * (z * jax.nn.sigmoid(z))).astype(BF16)
    h = x_ref[...] + jnp.dot(yg, wo_ref[...], preferred_element_type=F32)
    hn = h * lax.rsqrt(jnp.mean(h * h, axis=-1, keepdims=True) + EPS)
    hn = (hn * gple_ref[...]).astype(BF16)
    gate = jax.nn.sigmoid(jnp.dot(hn, wpg_ref[...], preferred_element_type=F32))
    ple = jnp.dot(p_ref[...].astype(BF16), wple_ref[...], preferred_element_type=F32)
    h = h + gate * ple
    if final:
        h = h * lax.rsqrt(jnp.mean(h * h, axis=-1, keepdims=True) + EPS)
        h = h * gfin_ref[...]
    out_ref[...] = h


def _outproj(x2, ya2, yb2, z2, p2, wo_b, wpg_b, wple_b, g_ple, g_final, final):
    R, D = x2.shape
    ts = ROW_TILE
    row = lambda w: pl.BlockSpec((ts, w), lambda i: (i, 0))
    full = lambda a: pl.BlockSpec(a.shape, lambda i: (0, 0))
    gp, gf = g_ple.reshape(1, D), g_final.reshape(1, D)
    work = 2 * (ts * (2 * D + 2 * D_MIX + p2.shape[1]) * 4 + (D_MIX * D + D * D + p2.shape[1] * D) * 2)
    return pl.pallas_call(
        functools.partial(_outproj_kernel, final=final),
        grid=(R // ts,),
        in_specs=[row(D), row(W_A), row(W_B), row(D_MIX), row(p2.shape[1]), full(wo_b), full(wpg_b), full(wple_b),
                  full(gp), full(gf)],
        out_specs=row(D),
        out_shape=jax.ShapeDtypeStruct((R, D), F32),
        compiler_params=pltpu.CompilerParams(dimension_semantics=("arbitrary",), vmem_limit_bytes=_vmem_limit(work)),
        name="outproj",
    )(x2, ya2, yb2, z2, p2, wo_b, wpg_b, wple_b, gp, gf)


def kernel(x_prompt, x_sample, cache_diff_k, cache_diff_v, cache_moba_k, cache_moba_v, page_table, p_prompt, p_sample, w_in, w_out, g_mix, lam, subln_g, rel_bias, w_ple, w_pg, g_ple, g_final):
    depth = w_in.shape[0]
    B, S, D = x_prompt.shape
    NS, T, _ = x_sample.shape
    n_pages = page_table.shape[1]
    page = cache_diff_k.shape[2]
    past = n_pages * page
    g_pages = min(PAGES_PER_STEP, n_pages)
    assert S % ATT_TILE == 0 and ATT_TILE == MOBA_BLOCK == ROW_TILE and W_A == W_B == N_STREAMS * LANES
    assert (B * S) % ROW_TILE == 0 and (NS * T) % ROW_TILE == 0
    assert past % MOBA_BLOCK == 0 and MOBA_BLOCK % page == 0 and T <= page and T <= NEW_PAD
    assert n_pages % g_pages == 0 and (g_pages * page) % MOBA_BLOCK == 0 and (PAGES_PER_DOT * page) % MOBA_BLOCK == 0
    assert MOBA_TOPK <= past // MOBA_BLOCK <= LANES and ATT_TILE >= MAX_DIST

    tabp = ((rel_bias - rel_bias[N_BUCKETS - 1:N_BUCKETS, :]) * LOG2E).T.astype(F32)
    tab_a, tab_b = tabp[:H_A], tabp[H_A:]
    near_d, new_d = _sample_bias_tables(tab_a, 2, T, page)
    dot_pages = PAGES_PER_DOT if g_pages % PAGES_PER_DOT == 0 else g_pages
    near_d = jnp.pad(near_d, ((0, 0), (dot_pages * page - page, 0)))
    near_m, new_m = _sample_bias_tables(tab_b, 1, T, MOBA_BLOCK)

    hp, hs = x_prompt, x_sample
    outs = [[] for _ in range(8)]
    for layer in range(depth):
        lam_init = 0.8 - 0.6 * math.exp(-0.3 * layer)
        final = layer == depth - 1
        w_in_b = w_in[layer].astype(BF16)
        wo_b, wpg_b, wple_b = w_out[layer].astype(BF16), w_pg[layer].astype(BF16), w_ple[layer].astype(BF16)

        (ka, va, kb, vb, z, qa_f, qb_f, ka_b, va_b, kb_b, vb_b, km) = _inproj(
            hp.reshape(B * S, D), g_mix[layer], w_in_b)
        r3 = lambda a: a.reshape(B, S, a.shape[-1])
        ya = _prompt_attn(r3(qa_f), r3(ka_b), r3(va_b), tab_a, moba=False, lam_init=lam_init,
                          lam=lam[layer], subln_g=subln_g[layer])
        yb = _prompt_attn(r3(qb_f), r3(kb_b), r3(vb_b), tab_b, moba=True, lam_init=lam_init,
                          kmean=km.reshape(B, S // MOBA_BLOCK, W_B))
        hp = _outproj(hp.reshape(B * S, D), ya.reshape(B * S, W_A), yb.reshape(B * S, W_B), z,
                      p_prompt[layer].reshape(B * S, -1), wo_b, wpg_b, wple_b,
                      g_ple[layer], g_final, final).reshape(B, S, D)
        outs[0].append(ka.reshape(B, S, H_A, 2 * DH_A)); outs[1].append(va.reshape(B, S, H_A, 2 * DH_A))
        outs[2].append(kb.reshape(B, S, H_B, DH_B)); outs[3].append(vb.reshape(B, S, H_B, DH_B))

        (ka, va, kb, vb, z, qa_f, qb_f, _, _, _, _, _) = _inproj(hs.reshape(NS * T, D), g_mix[layer], w_in_b)
        s3 = lambda a: a.reshape(NS, T, a.shape[-1])
        ya = _sample_diff(_block_diag_q(s3(qa_f).astype(BF16), H_A), page_table, cache_diff_k[layer],
                          cache_diff_v[layer], s3(ka), s3(va), near_d, new_d, lam[layer], subln_g[layer], lam_init)
        yb = _sample_moba(_block_diag_q(s3(qb_f).astype(BF16), H_B), _block_diag_q(s3(qb_f), H_B), page_table,
                          cache_moba_k[layer], cache_moba_v[layer], s3(kb), s3(vb), near_m, new_m)
        hs = _outproj(hs.reshape(NS * T, D), ya.reshape(NS * T, W_A), yb.reshape(NS * T, W_B), z,
                      p_sample[layer].reshape(NS * T, -1), wo_b, wpg_b, wple_b,
                      g_ple[layer], g_final, final).reshape(NS, T, D)
        outs[4].append(ka.reshape(NS, T, H_A, 2 * DH_A)); outs[5].append(va.reshape(NS, T, H_A, 2 * DH_A))
        outs[6].append(kb.reshape(NS, T, H_B, DH_B)); outs[7].append(vb.reshape(NS, T, H_B, DH_B))

    return (hp, hs) + tuple(jnp.stack(o) for o in outs)
```
